```python
import math
import jax, jax.numpy as jnp
from jax import lax
import numpy as np

D_MODEL = 1024
BATCH = 16
SEQ = 2048
DEPTH = 4

HEAD_DIM = 64
SB_HEADS = 4
MLA_HEADS = 4
RWKV_HEADS = 8
SB_WIDTH = SB_HEADS * HEAD_DIM
MLA_WIDTH = MLA_HEADS * HEAD_DIM
RWKV_WIDTH = RWKV_HEADS * HEAD_DIM
MIX_WIDTH = SB_WIDTH + MLA_WIDTH + RWKV_WIDTH

MLA_Q_LORA = 192
MLA_KV_LORA = 128
MLA_NOPE_DIM = 64
MLA_ROPE_DIM = 32
MLA_V_DIM = HEAD_DIM
ROPE_THETA = 10000.0

RWKV_DECAY_LORA = 64
RWKV_AAA_LORA = 64
RWKV_GATE_LORA = 128
RWKV_GN_EPS = 64e-5

SB_COLS = 3 * SB_WIDTH
MLA_COLS = MLA_Q_LORA + MLA_KV_LORA + MLA_ROPE_DIM
RWKV_COLS = 3 * RWKV_WIDTH + RWKV_DECAY_LORA + RWKV_AAA_LORA + RWKV_GATE_LORA
IN_COLS = SB_COLS + MLA_COLS + RWKV_COLS

D_FF = 2816
CONV_WIDTH = 3

Q_BLOCK = 128
LN_EPS = 1e-5
RMS_EPS = 1e-6
DEEPNORM_ALPHA = (2 * DEPTH) ** 0.25
DEEPNORM_BETA = (8 * DEPTH) ** -0.25

kernel_name = "hymba_style_sb_mla_rwkv7_deepnorm_convffn"


def layer_norm(x, g, b):
    xf = x.astype(jnp.float32)
    mu = jnp.mean(xf, axis=-1, keepdims=True)
    var = jnp.mean(jnp.square(xf - mu), axis=-1, keepdims=True)
    return ((xf - mu) * lax.rsqrt(var + LN_EPS) * g + b).astype(x.dtype)


def rms_norm(x, g):
    xf = x.astype(jnp.float32)
    return (xf * lax.rsqrt(jnp.mean(jnp.square(xf), axis=-1, keepdims=True) + RMS_EPS) * g).astype(x.dtype)


def split_heads(t, n_heads):
    b, s, _ = t.shape
    return t.reshape(b, s, n_heads, -1).transpose(0, 2, 1, 3)


def merge_heads(t):
    b, h, s, d = t.shape
    return t.transpose(0, 2, 1, 3).reshape(b, s, h * d)


def query_block_map(block_fn, q):
    b, h, s, d = q.shape
    nb = s // Q_BLOCK
    qb = q.reshape(b, h, nb, Q_BLOCK, d).transpose(2, 0, 1, 3, 4)
    starts = jnp.arange(nb, dtype=jnp.int32) * Q_BLOCK
    out = lax.map(lambda a: block_fn(a[0], a[1]), (qb, starts))
    return out.transpose(1, 2, 0, 3, 4).reshape(b, h, s, -1)


def stick_breaking_attention(q, k, v):
    s_len = k.shape[2]
    scale = q.shape[-1] ** -0.5
    kpos = jnp.arange(s_len, dtype=jnp.int32)

    def block(qb, start):
        z = jnp.einsum('bhqd,bhkd->bhqk', qb, k).astype(jnp.float32) * scale
        qpos = start + jnp.arange(Q_BLOCK, dtype=jnp.int32)
        past = kpos[None, :] < qpos[:, None]
        log_keep = jnp.where(past, jax.nn.log_sigmoid(-z), 0.0)
        log_later = lax.cumsum(log_keep, axis=3, reverse=True) - log_keep
        w = jnp.where(past, jnp.exp(jax.nn.log_sigmoid(z) + log_later), 0.0)
        return jnp.einsum('bhqk,bhkd->bhqd', w.astype(v.dtype), v)

    return query_block_map(block, q)


def causal_softmax_attention(q, k, v, scale):
    s_len = k.shape[2]
    kpos = jnp.arange(s_len, dtype=jnp.int32)
    neg = jnp.finfo(jnp.float32).min

    def block(qb, start):
        s = jnp.einsum('bhqd,bhkd->bhqk', qb, k).astype(jnp.float32) * scale
        qpos = start + jnp.arange(Q_BLOCK, dtype=jnp.int32)
        s = jnp.where(kpos[None, :] <= qpos[:, None], s, neg)
        p = jax.nn.softmax(s, axis=-1)
        return jnp.einsum('bhqk,bhkd->bhqd', p.astype(v.dtype), v)

    return query_block_map(block, q)


def rope_tables(s_len, dim):
    inv_freq = 1.0 / (ROPE_THETA ** (jnp.arange(0, dim, 2, dtype=jnp.float32) / dim))
    ang = jnp.arange(s_len, dtype=jnp.float32)[:, None] * inv_freq[None, :]
    return jnp.cos(ang), jnp.sin(ang)


def apply_rope(t, cos, sin):
    t1, t2 = jnp.split(t, 2, axis=-1)
    cos = cos.astype(t.dtype)
    sin = sin.astype(t.dtype)
    return jnp.concatenate([t1 * cos - t2 * sin, t2 * cos + t1 * sin], axis=-1)


def mla_mixer(cols, q_norm, w_uq, kv_norm, w_ukv):
    c_q, c_kv, k_rope = jnp.split(cols, [MLA_Q_LORA, MLA_Q_LORA + MLA_KV_LORA], axis=-1)
    q = split_heads(rms_norm(c_q, q_norm) @ w_uq, MLA_HEADS)
    kv = split_heads(rms_norm(c_kv, kv_norm) @ w_ukv, MLA_HEADS)
    q_nope, q_rope = jnp.split(q, [MLA_NOPE_DIM], axis=-1)
    k_nope, v = jnp.split(kv, [MLA_NOPE_DIM], axis=-1)
    cos, sin = rope_tables(cols.shape[1], MLA_ROPE_DIM)
    q_rope = apply_rope(q_rope, cos, sin)
    k_rope = apply_rope(k_rope[:, None], cos, sin)
    q = jnp.concatenate([q_nope, q_rope], axis=-1)
    k = jnp.concatenate([k_nope, jnp.broadcast_to(k_rope, k_nope.shape[:-1] + (MLA_ROPE_DIM,))], axis=-1)
    out = causal_softmax_attention(q, k, v, (MLA_NOPE_DIM + MLA_ROPE_DIM) ** -0.5)
    return merge_heads(out)


def token_shift_mix(p, mu):
    prev = jnp.pad(p, ((0, 0), (1, 0), (0, 0)))[:, :-1]
    return p + (prev - p) * mu


def rwkv7_mixer(cols, mu, w0, w2, a0, a2, g2, k_k, k_a, r_k, gn_g, gn_b):
    b, s, _ = cols.shape
    p = token_shift_mix(cols, mu)
    idx = [RWKV_WIDTH, 2 * RWKV_WIDTH, 3 * RWKV_WIDTH,
           3 * RWKV_WIDTH + RWKV_DECAY_LORA, 3 * RWKV_WIDTH + RWKV_DECAY_LORA + RWKV_AAA_LORA]
    r, k, v, wd, ad, gd = jnp.split(p, idx, axis=-1)
    w = -jax.nn.softplus(-(w0 + jnp.tanh(wd) @ w2)) - 0.5
    decay = jnp.exp(-jnp.exp(w.astype(jnp.float32)))
    a = jax.nn.sigmoid(a0 + ad @ a2)
    g = jax.nn.sigmoid(gd) @ g2
    kk = (k * k_k).reshape(b, s, RWKV_HEADS, HEAD_DIM).astype(jnp.float32)
    kk = kk * lax.rsqrt(jnp.maximum(jnp.sum(jnp.square(kk), axis=-1, keepdims=True), 1e-12))
    k = k * (1.0 + (a - 1.0) * k_a)

    def heads(t):
        return t.reshape(b, s, RWKV_HEADS, HEAD_DIM).astype(jnp.float32)

    rh, kh, vh, ah, wh = heads(r), heads(k), heads(v), heads(a), heads(decay)
    tm = lambda t: jnp.swapaxes(t, 0, 1)

    def step(state, inp):
        r_t, w_t, k_t, v_t, kk_t, a_t = inp
        sa = jnp.einsum('bhvk,bhk->bhv', state, -kk_t)
        state = (state * w_t[:, :, None, :] + sa[..., None] * (kk_t * a_t)[:, :, None, :]
                 + v_t[..., None] * k_t[:, :, None, :])
        return state, jnp.einsum('bhvk,bhk->bhv', state, r_t)

    init = jnp.zeros((b, RWKV_HEADS, HEAD_DIM, HEAD_DIM), jnp.float32)
    _, ys = lax.scan(step, init, (tm(rh), tm(wh), tm(kh), tm(vh), tm(kk), tm(ah)))
    y = jnp.swapaxes(ys, 0, 1)
    mu_y = jnp.mean(y, axis=-1, keepdims=True)
    var_y = jnp.mean(jnp.square(y - mu_y), axis=-1, keepdims=True)
    y = ((y - mu_y) * lax.rsqrt(var_y + RWKV_GN_EPS)).reshape(b, s, RWKV_WIDTH) * gn_g + gn_b
    bonus = jnp.sum(rh * kh * r_k, axis=-1, keepdims=True) * vh
    y = (y + bonus.reshape(b, s, RWKV_WIDTH)) * g
    return y.astype(cols.dtype)


def causal_depthwise_conv(u, w, bias):
    out = lax.conv_general_dilated(u, w[:, None, :], window_strides=(1,),
                                   padding=[(CONV_WIDTH - 1, 0)],
                                   dimension_numbers=('NWC', 'WIO', 'NWC'),
                                   feature_group_count=u.shape[-1])
    return out + bias


def hybrid_layer(x, w_in, mla_q_norm, mla_w_uq, mla_kv_norm, mla_w_ukv,
                 rwkv_mu, rwkv_w0, rwkv_w2, rwkv_a0, rwkv_a2, rwkv_g2, rwkv_k_k, rwkv_k_a, rwkv_r_k,
                 rwkv_gn_g, rwkv_gn_b, w_o, ln1_g, ln1_b,
                 ffn_w_up, ffn_conv_w, ffn_conv_b, ffn_w_down, ln2_g, ln2_b):
    h = x @ w_in
    sb_cols, mla_cols, rwkv_cols = jnp.split(h, [SB_COLS, SB_COLS + MLA_COLS], axis=-1)
    q, k, v = (split_heads(t, SB_HEADS) for t in jnp.split(sb_cols, 3, axis=-1))
    o_sb = merge_heads(stick_breaking_attention(q, k, v))
    o_mla = mla_mixer(mla_cols, mla_q_norm, mla_w_uq, mla_kv_norm, mla_w_ukv)
    o_rwkv = rwkv7_mixer(rwkv_cols, rwkv_mu, rwkv_w0, rwkv_w2, rwkv_a0, rwkv_a2, rwkv_g2,
                         rwkv_k_k, rwkv_k_a, rwkv_r_k, rwkv_gn_g, rwkv_gn_b)
    mix = jnp.concatenate([o_sb, o_mla, o_rwkv], axis=-1) @ w_o
    x = layer_norm(DEEPNORM_ALPHA * x + mix, ln1_g, ln1_b)
    u_act, u_gate = jnp.split(x @ ffn_w_up, 2, axis=-1)
    hid = jax.nn.gelu(causal_depthwise_conv(u_act, ffn_conv_w, ffn_conv_b), approximate=False) * u_gate
    x = layer_norm(DEEPNORM_ALPHA * x + hid @ ffn_w_down, ln2_g, ln2_b)
    return x


def setup_inputs(seed: int = 0) -> dict:
    key = jax.random.key(seed)
    ks = iter(jax.random.split(key, 32))
    L = DEPTH

    def nrm(shape, scale):
        return jax.random.normal(next(ks), shape, jnp.float32) * scale

    def unif(shape, lo, hi):
        return jax.random.uniform(next(ks), shape, jnp.float32, lo, hi)

    return {
        "x": nrm((BATCH, SEQ, D_MODEL), 1.0),
        "w_in": nrm((L, D_MODEL, IN_COLS), D_MODEL ** -0.5),
        "mla_q_norm": 1.0 + nrm((L, MLA_Q_LORA), 0.02),
        "mla_w_uq": nrm((L, MLA_Q_LORA, MLA_HEADS * (MLA_NOPE_DIM + MLA_ROPE_DIM)), MLA_Q_LORA ** -0.5),
        "mla_kv_norm": 1.0 + nrm((L, MLA_KV_LORA), 0.02),
        "mla_w_ukv": nrm((L, MLA_KV_LORA, MLA_HEADS * (MLA_NOPE_DIM + MLA_V_DIM)), MLA_KV_LORA ** -0.5),
        "rwkv_mu": unif((L, RWKV_COLS), 0.0, 1.0),
        "rwkv_w0": unif((L, RWKV_WIDTH), -6.5, -1.5),
        "rwkv_w2": nrm((L, RWKV_DECAY_LORA, RWKV_WIDTH), 0.1),
        "rwkv_a0": nrm((L, RWKV_WIDTH), 0.1),
        "rwkv_a2": nrm((L, RWKV_AAA_LORA, RWKV_WIDTH), RWKV_AAA_LORA ** -0.5),
        "rwkv_g2": nrm((L, RWKV_GATE_LORA, RWKV_WIDTH), RWKV_GATE_LORA ** -0.5),
        "rwkv_k_k": 0.85 + nrm((L, RWKV_WIDTH), 0.05),
        "rwkv_k_a": 1.0 + nrm((L, RWKV_WIDTH), 0.05),
        "rwkv_r_k": nrm((L, RWKV_HEADS, HEAD_DIM), 0.1),
        "rwkv_gn_g": 1.0 + nrm((L, RWKV_WIDTH), 0.02),
        "rwkv_gn_b": nrm((L, RWKV_WIDTH), 0.02),
        "w_o": nrm((L, MIX_WIDTH, D_MODEL), MIX_WIDTH ** -0.5 * DEEPNORM_BETA),
        "ln1_g": 1.0 + nrm((L, D_MODEL), 0.02),
        "ln1_b": nrm((L, D_MODEL), 0.02),
        "ffn_w_up": nrm((L, D_MODEL, 2 * D_FF), D_MODEL ** -0.5),
        "ffn_conv_w": nrm((L, CONV_WIDTH, D_FF), CONV_WIDTH ** -0.5),
        "ffn_conv_b": nrm((L, D_FF), 0.02),
        "ffn_w_down": nrm((L, D_FF, D_MODEL), D_FF ** -0.5 * DEEPNORM_BETA),
        "ln2_g": 1.0 + nrm((L, D_MODEL), 0.02),
        "ln2_b": nrm((L, D_MODEL), 0.02),
    }


def reference(x, w_in, mla_q_norm, mla_w_uq, mla_kv_norm, mla_w_ukv,
              rwkv_mu, rwkv_w0, rwkv_w2, rwkv_a0, rwkv_a2, rwkv_g2, rwkv_k_k, rwkv_k_a, rwkv_r_k,
              rwkv_gn_g, rwkv_gn_b, w_o, ln1_g, ln1_b,
              ffn_w_up, ffn_conv_w, ffn_conv_b, ffn_w_down, ln2_g, ln2_b):
    for l in range(DEPTH):
        x = hybrid_layer(x, w_in[l], mla_q_norm[l], mla_w_uq[l], mla_kv_norm[l], mla_w_ukv[l],
                         rwkv_mu[l], rwkv_w0[l], rwkv_w2[l], rwkv_a0[l], rwkv_a2[l], rwkv_g2[l],
                         rwkv_k_k[l], rwkv_k_a[l], rwkv_r_k[l], rwkv_gn_g[l], rwkv_gn_b[l],
                         w_o[l], ln1_g[l], ln1_b[l],
                         ffn_w_up[l], ffn_conv_w[l], ffn_conv_b[l], ffn_w_down[l], ln2_g[l], ln2_b[l])
    return x
```

```python
import functools
import math

import jax
import jax.numpy as jnp
from jax import lax
from jax.experimental import pallas as pl
from jax.experimental.pallas import tpu as pltpu

F32 = jnp.float32
BF16 = jnp.bfloat16

HEAD_DIM = 64
SB_HEADS = 4
MLA_HEADS = 4
RWKV_HEADS = 8
SB_WIDTH = SB_HEADS * HEAD_DIM
MLA_WIDTH = MLA_HEADS * HEAD_DIM
RWKV_WIDTH = RWKV_HEADS * HEAD_DIM
MLA_Q_LORA = 192
MLA_KV_LORA = 128
MLA_NOPE_DIM = 64
MLA_ROPE_DIM = 32
MLA_PAD_DIM = 128
ROPE_THETA = 10000.0
RWKV_DECAY_LORA = 64
RWKV_AAA_LORA = 64
RWKV_GATE_LORA = 128
RWKV_GN_EPS = 64e-5
SB_COLS = 3 * SB_WIDTH
MLA_COLS = MLA_Q_LORA + MLA_KV_LORA + MLA_ROPE_DIM
RWKV_COLS = 3 * RWKV_WIDTH + RWKV_DECAY_LORA + RWKV_AAA_LORA + RWKV_GATE_LORA
CONV_WIDTH = 3
LN_EPS = 1e-5
RMS_EPS = 1e-6

LANES = 128
SUBLANES = 8
VMEM_LIMIT = 48 * 1024 * 1024


def _dot(a, b):
    return jnp.dot(a, b, preferred_element_type=F32)


def _dot_nt(a, b):
    return lax.dot_general(a, b, (((1,), (1,)), ((), ())), preferred_element_type=F32)


def _split_bf16(x, terms):
    out = []
    rem = x
    for _ in range(terms):
        part = rem.astype(BF16)
        out.append(part)
        rem = rem - part.astype(F32)
    return out


def _params(sem):
    return pltpu.CompilerParams(dimension_semantics=sem, vmem_limit_bytes=VMEM_LIMIT)


def _inproj_kernel(x_ref, wsb_ref, wcq_ref, wckv_ref, wkr_ref, wrw_ref,
                   q_ref, k_ref, v_ref, cq_ref, ckv_ref, kr_ref, hr_ref):
    xb = x_ref[...].astype(BF16)
    sb = _dot(xb, wsb_ref[...])
    q_ref[...] = sb[:, :SB_WIDTH].astype(BF16)
    k_ref[...] = sb[:, SB_WIDTH:2 * SB_WIDTH].astype(BF16)
    v_ref[...] = sb[:, 2 * SB_WIDTH:].astype(BF16)
    cq_ref[...] = _dot(xb, wcq_ref[...])
    ckv_ref[...] = _dot(xb, wckv_ref[...])
    kr_ref[...] = _dot(xb, wkr_ref[...])
    hr_ref[...] = _dot(xb, wrw_ref[...])


def _inproj(x2d, wsb, wcq, wckv, wkr, wrw, tm):
    t, d = x2d.shape
    row = lambda n: pl.BlockSpec((tm, n), lambda i: (i, 0))
    full = lambda w: pl.BlockSpec(w.shape, lambda i: (0, 0))
    widths = (SB_WIDTH, SB_WIDTH, SB_WIDTH, MLA_Q_LORA, MLA_KV_LORA, 2 * MLA_ROPE_DIM, RWKV_COLS)
    dtypes = (BF16, BF16, BF16, F32, F32, F32, F32)
    return pl.pallas_call(
        _inproj_kernel,
        grid=(t // tm,),
        in_specs=[row(d), full(wsb), full(wcq), full(wckv), full(wkr), full(wrw)],
        out_specs=[row(n) for n in widths],
        out_shape=[jax.ShapeDtypeStruct((t, n), dt) for n, dt in zip(widths, dtypes)],
        compiler_params=_params(("parallel",)),
        name="inproj",
    )(x2d, wsb, wcq, wckv, wkr, wrw)


def _sb_attn_kernel(q_ref, k_ref, v_ref, o_ref, *, blk, scale):
    i = pl.program_id(2)
    q = q_ref[0]
    lane = lax.broadcasted_iota(jnp.int32, (blk, LANES), 1)
    row = lax.broadcasted_iota(jnp.int32, (blk, blk), 0)
    col = lax.broadcasted_iota(jnp.int32, (blk, blk), 1)
    upper = jnp.where(row > col, 1.0, 0.0).astype(BF16)
    past_diag = col < row
    head_masks = [lane < HEAD_DIM, lane >= HEAD_DIM]
    qh = [jnp.where(m, q, jnp.zeros_like(q)) for m in head_masks]

    def block(j, carry, diag):
        start = pl.multiple_of(j * blk, blk)
        kb = k_ref[0, pl.ds(start, blk), :]
        vb = v_ref[0, pl.ds(start, blk), :]
        new = []
        for h in range(2):
            acc, run = carry[2 * h], carry[2 * h + 1]
            z = _dot_nt(qh[h], kb) * scale
            log_keep = jax.nn.log_sigmoid(-z)
            if diag:
                log_keep = jnp.where(past_diag, log_keep, 0.0)
            hi, lo = _split_bf16(log_keep, 2)
            later = _dot(hi, upper) + _dot(lo, upper)
            logw = z + log_keep + later + run
            w = jnp.exp(logw)
            if diag:
                w = jnp.where(past_diag, w, 0.0)
            acc = acc + _dot(w.astype(BF16), vb)
            run = run + jnp.sum(log_keep, axis=1, keepdims=True)
            new += [acc, run]
        return tuple(new)

    zero_acc = jnp.zeros((blk, LANES), F32)
    zero_run = jnp.zeros((blk, 1), F32)
    carry = block(i, (zero_acc, zero_run, zero_acc, zero_run), True)
    carry = lax.fori_loop(0, i, lambda s, c: block(i - 1 - s, c, False), carry)
    o_ref[0] = jnp.where(head_masks[0], carry[0], carry[2]).astype(o_ref.dtype)


def _sb_attention(q, k, v, blk):
    b, s, w = q.shape
    pairs = w // LANES
    kern = functools.partial(_sb_attn_kernel, blk=blk, scale=HEAD_DIM ** -0.5)
    return pl.pallas_call(
        kern,
        grid=(b, pairs, s // blk),
        in_specs=[pl.BlockSpec((1, blk, LANES), lambda bi, p, i: (bi, i, p)),
                  pl.BlockSpec((1, s, LANES), lambda bi, p, i: (bi, 0, p)),
                  pl.BlockSpec((1, s, LANES), lambda bi, p, i: (bi, 0, p))],
        out_specs=pl.BlockSpec((1, blk, LANES), lambda bi, p, i: (bi, i, p)),
        out_shape=jax.ShapeDtypeStruct((b, s, w), BF16),
        compiler_params=_params(("parallel", "parallel", "arbitrary")),
        name="sb_attn",
    )(q, k, v)


def _mla_prep_kernel(cq_ref, ckv_ref, kr_ref, qn_ref, kvn_ref, wq_ref, wqr_ref, wkn_ref, wv_ref, scat_ref,
                     qcos_ref, qsin_ref, ktab_ref, q_out, k_out, v_out):
    cq = cq_ref[...]
    cq = cq * lax.rsqrt(jnp.mean(jnp.square(cq), axis=-1, keepdims=True) + RMS_EPS) * qn_ref[...]
    cqb = cq.astype(BF16)
    q = _dot(cqb, wq_ref[...]) * qcos_ref[...] + _dot(cqb, wqr_ref[...]) * qsin_ref[...]
    q_out[...] = q.astype(BF16)
    ckv = ckv_ref[...]
    ckv = ckv * lax.rsqrt(jnp.mean(jnp.square(ckv), axis=-1, keepdims=True) + RMS_EPS) * kvn_ref[...]
    ckvb = ckv.astype(BF16)
    v_out[...] = _dot(ckvb, wv_ref[...]).astype(BF16)
    prod = kr_ref[...] * ktab_ref[...]
    hi, lo = _split_bf16(prod, 2)
    rope = _dot(hi, scat_ref[...]) + _dot(lo, scat_ref[...])
    k_out[...] = (_dot(ckvb, wkn_ref[...]) + rope).astype(BF16)


def _mla_prep(cq, ckv, kr, qn, kvn, wq, wqr, wkn, wv, scat, qcos, qsin, ktab, s, tm):
    t = cq.shape[0]
    nj = s // tm
    row = lambda n: pl.BlockSpec((tm, n), lambda i: (i, 0))
    full = lambda w: pl.BlockSpec(w.shape, lambda i: (0, 0))
    pos = lambda n: pl.BlockSpec((tm, n), lambda i: (i % nj, 0))
    qk_w = MLA_HEADS * MLA_PAD_DIM
    return pl.pallas_call(
        _mla_prep_kernel,
        grid=(t // tm,),
        in_specs=[row(MLA_Q_LORA), row(MLA_KV_LORA), row(2 * MLA_ROPE_DIM), full(qn), full(kvn), full(wq),
                  full(wqr), full(wkn), full(wv), full(scat), pos(qk_w), pos(qk_w), pos(2 * MLA_ROPE_DIM)],
        out_specs=[row(qk_w), row(qk_w), row(MLA_WIDTH)],
        out_shape=[jax.ShapeDtypeStruct((t, qk_w), BF16), jax.ShapeDtypeStruct((t, qk_w), BF16),
                   jax.ShapeDtypeStruct((t, MLA_WIDTH), BF16)],
        compiler_params=_params(("parallel",)),
        name="mla_prep",
    )(cq, ckv, kr, qn, kvn, wq, wqr, wkn, wv, scat, qcos, qsin, ktab)


def _mla_attn_kernel(q_ref, k_ref, v_ref, o_ref, *, blk, scale):
    i = pl.program_id(2)
    q = q_ref[0]
    qh = [q[:, :MLA_PAD_DIM], q[:, MLA_PAD_DIM:]]
    row = lax.broadcasted_iota(jnp.int32, (blk, blk), 0)
    col = lax.broadcasted_iota(jnp.int32, (blk, blk), 1)
    causal_diag = col <= row
    lane = lax.broadcasted_iota(jnp.int32, (blk, LANES), 1)

    def block(j, carry, diag):
        start = pl.multiple_of(j * blk, blk)
        kb = k_ref[0, pl.ds(start, blk), :]
        vb = v_ref[0, pl.ds(start, blk), :]
        new = []
        for h in range(2):
            m, l, acc = carry[3 * h:3 * h + 3]
            sc = _dot_nt(qh[h], kb[:, h * MLA_PAD_DIM:(h + 1) * MLA_PAD_DIM]) * scale
            if diag:
                sc = jnp.where(causal_diag, sc, -jnp.inf)
            m_new = jnp.maximum(m, jnp.max(sc, axis=1, keepdims=True))
            alpha = jnp.exp(m - m_new)
            p = jnp.exp(sc - m_new)
            l = alpha * l + jnp.sum(p, axis=1, keepdims=True)
            acc = alpha * acc + _dot(p.astype(BF16), vb)
            new += [m_new, l, acc]
        return tuple(new)

    init = (jnp.full((blk, 1), -jnp.inf, F32), jnp.zeros((blk, 1), F32), jnp.zeros((blk, LANES), F32)) * 2
    carry = lax.fori_loop(0, i, lambda j, c: block(j, c, False), init)
    carry = block(i, carry, True)
    out = jnp.where(lane < HEAD_DIM, carry[2] / carry[1], carry[5] / carry[4])
    o_ref[0] = out.astype(o_ref.dtype)


def _mla_attention(q, k, v, blk):
    b, s, _ = q.shape
    pairs = MLA_HEADS // 2
    kern = functools.partial(_mla_attn_kernel, blk=blk, scale=(MLA_NOPE_DIM + MLA_ROPE_DIM) ** -0.5)
    return pl.pallas_call(
        kern,
        grid=(b, pairs, s // blk),
        in_specs=[pl.BlockSpec((1, blk, 2 * MLA_PAD_DIM), lambda bi, p, i: (bi, i, p)),
                  pl.BlockSpec((1, s, 2 * MLA_PAD_DIM), lambda bi, p, i: (bi, 0, p)),
                  pl.BlockSpec((1, s, LANES), lambda bi, p, i: (bi, 0, p))],
        out_specs=pl.BlockSpec((1, blk, LANES), lambda bi, p, i: (bi, i, p)),
        out_shape=jax.ShapeDtypeStruct((b, s, MLA_WIDTH), BF16),
        compiler_params=_params(("parallel", "parallel", "arbitrary")),
        name="mla_attn",
    )(q, k, v)


def _rwkv_prep_kernel(h_ref, mu_ref, w0_ref, a0_ref, wlora_ref, g2_ref,
                      r_out, w_out, k_out, v_out, a_out, g_out, shift_ref):
    tm = h_ref.shape[1]
    c = RWKV_WIDTH

    @pl.when(pl.program_id(1) == 0)
    def _():
        shift_ref[0:SUBLANES, :] = jnp.zeros((SUBLANES, RWKV_COLS), F32)

    h = h_ref[0]
    shift_ref[SUBLANES:, :] = h
    prev = shift_ref[pl.ds(SUBLANES - 1, tm), :]
    shift_ref[0:SUBLANES, :] = h[tm - SUBLANES:, :]
    p = h + (prev - h) * mu_ref[...]
    r_out[0] = p[:, :c]
    k_out[0] = p[:, c:2 * c]
    v_out[0] = p[:, 2 * c:3 * c]
    lora_in = p[:, 3 * c:3 * c + LANES]
    lane = lax.broadcasted_iota(jnp.int32, lora_in.shape, 1)
    lora_in = jnp.where(lane < RWKV_DECAY_LORA, jnp.tanh(lora_in), lora_in)
    lora = _dot(lora_in.astype(BF16), wlora_ref[...])
    w = -jax.nn.softplus(-(w0_ref[...] + lora[:, :c])) - 0.5
    w_out[0] = jnp.exp(-jnp.exp(w))
    a_out[0] = jax.nn.sigmoid(a0_ref[...] + lora[:, c:])
    gate_in = jax.nn.sigmoid(p[:, 3 * c + LANES:])
    g_out[0] = _dot(gate_in.astype(BF16), g2_ref[...])


def _rwkv_prep(hr, mu, w0, a0, wlora, g2, tm):
    b, s, _ = hr.shape
    full = lambda w: pl.BlockSpec(w.shape, lambda bi, j: (0, 0))
    out = pl.BlockSpec((1, tm, RWKV_WIDTH), lambda bi, j: (bi, j, 0))
    return pl.pallas_call(
        _rwkv_prep_kernel,
        grid=(b, s // tm),
        in_specs=[pl.BlockSpec((1, tm, RWKV_COLS), lambda bi, j: (bi, j, 0)),
                  full(mu), full(w0), full(a0), full(wlora), full(g2)],
        out_specs=[out] * 6,
        out_shape=[jax.ShapeDtypeStruct((b, s, RWKV_WIDTH), F32)] * 6,
        scratch_shapes=[pltpu.VMEM((tm + SUBLANES, RWKV_COLS), F32)],
        compiler_params=_params(("arbitrary", "arbitrary")),
        name="rwkv_prep",
    )(hr, mu, w0, a0, wlora, g2)


def _rwkv_scan_kernel(r_ref, w_ref, k_ref, v_ref, a_ref, kk_ref, ka_ref, rk_ref, gng_ref, gnb_ref,
                      y_ref, state_ref, nkk_ref, b_ref, kp_ref):
    n = HEAD_DIM
    tb = r_ref.shape[0]

    @pl.when(pl.program_id(0) == 0)
    def _():
        state_ref[...] = jnp.zeros(state_ref.shape, F32)

    def step(t, _):
        kt = k_ref[t]
        at = a_ref[t]
        vt = v_ref[t]
        kk = kt * kk_ref[...]
        kk = kk * lax.rsqrt(jnp.maximum(jnp.sum(jnp.square(kk), axis=0, keepdims=True), 1e-12))
        kp = kt * (1.0 + (at - 1.0) * ka_ref[...])
        nkk_ref[...] = -kk
        b_ref[...] = kk * at
        kp_ref[...] = kp
        sa = jnp.zeros((n, y_ref.shape[2]), F32)
        for c in range(n):
            sa = sa + state_ref[c] * nkk_ref[c:c + 1, :]
        y = jnp.zeros_like(sa)
        for c in range(n):
            s_new = (state_ref[c] * w_ref[t, c:c + 1, :] + sa * b_ref[c:c + 1, :]
                     + vt * kp_ref[c:c + 1, :])
            state_ref[c] = s_new
            y = y + s_new * r_ref[t, c:c + 1, :]
        mean = jnp.mean(y, axis=0, keepdims=True)
        var = jnp.mean(jnp.square(y - mean), axis=0, keepdims=True)
        y = (y - mean) * lax.rsqrt(var + RWKV_GN_EPS) * gng_ref[...] + gnb_ref[...]
        bonus = jnp.sum(r_ref[t] * kp * rk_ref[...], axis=0, keepdims=True) * vt
        y_ref[t] = y + bonus
        return 0

    lax.fori_loop(0, tb, step, 0)


def _rwkv_scan(r, w, k, v, a, kk_t, ka_t, rk_t, gng_t, gnb_t, tb):
    s, n, lanes = r.shape
    seq = pl.BlockSpec((tb, n, lanes), lambda i: (i, 0, 0))
    par = pl.BlockSpec((n, lanes), lambda i: (0, 0))
    return pl.pallas_call(
        _rwkv_scan_kernel,
        grid=(s // tb,),
        in_specs=[seq] * 5 + [par] * 5,
        out_specs=seq,
        out_shape=jax.ShapeDtypeStruct((s, n, lanes), F32),
        scratch_shapes=[pltpu.VMEM((n, n, lanes), F32)] + [pltpu.VMEM((n, lanes), F32)] * 3,
        compiler_params=_params(("arbitrary",)),
        name="rwkv_scan",
    )(r, w, k, v, a, kk_t, ka_t, rk_t, gng_t, gnb_t)


def _layer_norm(x, g, b):
    mu = jnp.mean(x, axis=-1, keepdims=True)
    xc = x - mu
    var = jnp.mean(jnp.square(xc), axis=-1, keepdims=True)
    return xc * lax.rsqrt(var + LN_EPS) * g + b


def _outproj_kernel(x_ref, sb_ref, mla_ref, y_ref, g_ref, wsb_ref, wmla_ref, wrw_ref, lng_ref, lnb_ref,
                    o_ref, *, alpha):
    mix = _dot(sb_ref[...], wsb_ref[...]) + _dot(mla_ref[...], wmla_ref[...])
    mix = mix + _dot((y_ref[...] * g_ref[...]).astype(BF16), wrw_ref[...])
    o_ref[...] = _layer_norm(alpha * x_ref[...] + mix, lng_ref[...], lnb_ref[...])


def _outproj(x2d, o_sb, o_mla, y, g, wsb, wmla, wrw, lng, lnb, alpha, tm):
    t, d = x2d.shape
    row = lambda n: pl.BlockSpec((tm, n), lambda i: (i, 0))
    full = lambda w: pl.BlockSpec(w.shape, lambda i: (0, 0))
    return pl.pallas_call(
        functools.partial(_outproj_kernel, alpha=alpha),
        grid=(t // tm,),
        in_specs=[row(d), row(SB_WIDTH), row(MLA_WIDTH), row(RWKV_WIDTH), row(RWKV_WIDTH),
                  full(wsb), full(wmla), full(wrw), full(lng), full(lnb)],
        out_specs=row(d),
        out_shape=jax.ShapeDtypeStruct((t, d), F32),
        compiler_params=_params(("parallel",)),
        name="outproj_ln",
    )(x2d, o_sb, o_mla, y, g, wsb, wmla, wrw, lng, lnb)


def _ffn_kernel(x_ref, wact_ref, wgate_ref, convw_ref, convb_ref, wdown_ref, lng_ref, lnb_ref,
                o_ref, hid_ref, shift_ref, carry_ref, *, alpha, fc):
    tm = x_ref.shape[1]
    d_ff = wact_ref.shape[1]

    @pl.when(pl.program_id(1) == 0)
    def _():
        carry_ref[...] = jnp.zeros(carry_ref.shape, F32)

    x = x_ref[0]
    xb = x.astype(BF16)
    for c in range(d_ff // fc):
        cols = slice(c * fc, (c + 1) * fc)
        u = _dot(xb, wact_ref[:, cols])
        shift_ref[0:SUBLANES, :] = carry_ref[:, cols]
        shift_ref[SUBLANES:, :] = u
        carry_ref[:, cols] = u[tm - SUBLANES:, :]
        prev1 = shift_ref[pl.ds(SUBLANES - 1, tm), :]
        prev2 = shift_ref[pl.ds(SUBLANES - 2, tm), :]
        conv = (prev2 * convw_ref[0:1, cols] + prev1 * convw_ref[1:2, cols] + u * convw_ref[2:3, cols]
                + convb_ref[:, cols])
        gate = _dot(xb, wgate_ref[:, cols])
        gelu = 0.5 * conv * (1.0 + lax.erf(conv * math.sqrt(0.5)))
        hid_ref[:, cols] = (gelu * gate).astype(BF16)
    down = _dot(hid_ref[...], wdown_ref[...])
    o_ref[0] = _layer_norm(alpha * x + down, lng_ref[...], lnb_ref[...])


def _ffn(x, wact, wgate, convw, convb, wdown, lng, lnb, alpha, tm, fc):
    b, s, d = x.shape
    d_ff = wact.shape[1]
    const = lambda w: pl.BlockSpec(w.shape, lambda bi, j: (0, 0), pipeline_mode=pl.Buffered(1))
    tile = pl.BlockSpec((1, tm, d), lambda bi, j: (bi, j, 0))
    return pl.pallas_call(
        functools.partial(_ffn_kernel, alpha=alpha, fc=fc),
        grid=(b, s // tm),
        in_specs=[tile, const(wact), const(wgate), const(convw), const(convb), const(wdown),
                  const(lng), const(lnb)],
        out_specs=tile,
        out_shape=jax.ShapeDtypeStruct((b, s, d), F32),
        scratch_shapes=[pltpu.VMEM((tm, d_ff), BF16),
                        pltpu.VMEM((tm + SUBLANES, fc), F32),
                        pltpu.VMEM((SUBLANES, d_ff), F32)],
        compiler_params=_params(("arbitrary", "arbitrary")),
        name="conv_ffn",
    )(x, wact, wgate, convw, convb, wdown, lng, lnb)


def _rope_tables(s):
    half = MLA_ROPE_DIM // 2
    inv_freq = 1.0 / (ROPE_THETA ** (jnp.arange(0, MLA_ROPE_DIM, 2, dtype=F32) / MLA_ROPE_DIM))
    ang = jnp.arange(s, dtype=F32)[:, None] * inv_freq[None, :]
    cos, sin = jnp.cos(ang), jnp.sin(ang)
    cos2 = jnp.concatenate([cos, cos], axis=-1)
    sin2 = jnp.concatenate([-sin, sin], axis=-1)
    pad = jnp.zeros((s, MLA_PAD_DIM - MLA_NOPE_DIM - MLA_ROPE_DIM), F32)
    qcos = jnp.tile(jnp.concatenate([jnp.ones((s, MLA_NOPE_DIM), F32), cos2, pad], axis=-1), (1, MLA_HEADS))
    qsin = jnp.tile(jnp.concatenate([jnp.zeros((s, MLA_NOPE_DIM), F32), sin2, pad], axis=-1), (1, MLA_HEADS))
    ktab = jnp.concatenate([cos2, sin2], axis=-1)
    del half
    return qcos, qsin, ktab


def _swap_halves(w):
    half = w.shape[-1] // 2
    return jnp.concatenate([w[..., half:], w[..., :half]], axis=-1)


def _mla_weights(w_uq, w_ukv):
    qd = MLA_NOPE_DIM + MLA_ROPE_DIM
    pad = MLA_PAD_DIM - qd
    uq = w_uq.reshape(MLA_Q_LORA, MLA_HEADS, qd)
    zq = jnp.zeros((MLA_Q_LORA, MLA_HEADS, pad), F32)
    wq = jnp.concatenate([uq, zq], axis=-1)
    wqr = jnp.concatenate([jnp.zeros((MLA_Q_LORA, MLA_HEADS, MLA_NOPE_DIM), F32),
                           _swap_halves(uq[..., MLA_NOPE_DIM:]), zq], axis=-1)
    ukv = w_ukv.reshape(MLA_KV_LORA, MLA_HEADS, MLA_NOPE_DIM + HEAD_DIM)
    wkn = jnp.concatenate([ukv[..., :MLA_NOPE_DIM],
                           jnp.zeros((MLA_KV_LORA, MLA_HEADS, MLA_PAD_DIM - MLA_NOPE_DIM), F32)], axis=-1)
    wv = ukv[..., MLA_NOPE_DIM:]
    eye = jnp.eye(MLA_ROPE_DIM, dtype=F32)[:, None, :]
    scat = jnp.concatenate([jnp.zeros((MLA_ROPE_DIM, MLA_HEADS, MLA_NOPE_DIM), F32),
                            jnp.broadcast_to(eye, (MLA_ROPE_DIM, MLA_HEADS, MLA_ROPE_DIM)),
                            jnp.zeros((MLA_ROPE_DIM, MLA_HEADS, pad), F32)], axis=-1)
    scat = jnp.concatenate([scat, scat], axis=0)
    flat = lambda w: w.reshape(w.shape[0], -1).astype(BF16)
    return flat(wq), flat(wqr), flat(wkn), flat(wv), flat(scat)


def _to_chain_layout(t, b, s):
    return t.reshape(b, s, RWKV_HEADS, HEAD_DIM).transpose(1, 3, 0, 2).reshape(s, HEAD_DIM, b * RWKV_HEADS)


def _from_chain_layout(t, b, s):
    return t.reshape(s, HEAD_DIM, b, RWKV_HEADS).transpose(2, 0, 3, 1).reshape(b, s, RWKV_WIDTH)


def _chain_param(p, b):
    return jnp.tile(p.reshape(RWKV_HEADS, HEAD_DIM).T, (1, b))


def _layer(x, w_in, mla_q_norm, mla_w_uq, mla_kv_norm, mla_w_ukv, rwkv_mu, rwkv_w0, rwkv_w2, rwkv_a0, rwkv_a2,
           rwkv_g2, rwkv_k_k, rwkv_k_a, rwkv_r_k, rwkv_gn_g, rwkv_gn_b, w_o, ln1_g, ln1_b,
           ffn_w_up, ffn_conv_w, ffn_conv_b, ffn_w_down, ln2_g, ln2_b, *, alpha, rope):
    b, s, d = x.shape
    t = b * s
    x2d = x.reshape(t, d)
    row_tile = min(512, s)
    attn_blk = min(256, s)

    c0 = SB_COLS
    c1 = c0 + MLA_Q_LORA
    c2 = c1 + MLA_KV_LORA
    c3 = c2 + MLA_ROPE_DIM
    w_kr = w_in[:, c2:c3]
    wkr2 = jnp.concatenate([w_kr, _swap_halves(w_kr)], axis=-1)
    bf = lambda w: w.astype(BF16)
    q_sb, k_sb, v_sb, cq, ckv, kr, hr = _inproj(
        x2d, bf(w_in[:, :c0]), bf(w_in[:, c0:c1]), bf(w_in[:, c1:c2]), bf(wkr2), bf(w_in[:, c3:]), row_tile)

    o_sb = _sb_attention(q_sb.reshape(b, s, -1), k_sb.reshape(b, s, -1), v_sb.reshape(b, s, -1), attn_blk)

    wq, wqr, wkn, wv, scat = _mla_weights(mla_w_uq, mla_w_ukv)
    qcos, qsin, ktab = rope
    q_m, k_m, v_m = _mla_prep(cq, ckv, kr, mla_q_norm[None, :], mla_kv_norm[None, :], wq, wqr, wkn, wv, scat,
                              qcos, qsin, ktab, s, row_tile)
    o_mla = _mla_attention(q_m.reshape(b, s, -1), k_m.reshape(b, s, -1), v_m.reshape(b, s, -1), attn_blk)

    zl = jnp.zeros((RWKV_DECAY_LORA, RWKV_WIDTH), F32)
    wlora = jnp.concatenate([jnp.concatenate([rwkv_w2, zl], axis=1),
                             jnp.concatenate([zl, rwkv_a2], axis=1)], axis=0)
    r, w, k, v, a, g = _rwkv_prep(hr.reshape(b, s, -1), rwkv_mu[None, :], rwkv_w0[None, :], rwkv_a0[None, :],
                                  bf(wlora), bf(rwkv_g2), row_tile)
    chain = lambda z: _to_chain_layout(z, b, s)
    y = _rwkv_scan(chain(r), chain(w), chain(k), chain(v), chain(a),
                   _chain_param(rwkv_k_k, b), _chain_param(rwkv_k_a, b), _chain_param(rwkv_r_k.reshape(-1), b),
                   _chain_param(rwkv_gn_g, b), _chain_param(rwkv_gn_b, b), min(32, s))
    y = _from_chain_layout(y, b, s)

    x1 = _outproj(x2d, o_sb.reshape(t, -1), o_mla.reshape(t, -1), y.reshape(t, -1), g.reshape(t, -1),
                  bf(w_o[:SB_WIDTH]), bf(w_o[SB_WIDTH:SB_WIDTH + MLA_WIDTH]), bf(w_o[SB_WIDTH + MLA_WIDTH:]),
                  ln1_g[None, :], ln1_b[None, :], alpha, row_tile)

    d_ff = ffn_w_down.shape[0]
    return _ffn(x1.reshape(b, s, d), bf(ffn_w_up[:, :d_ff]), bf(ffn_w_up[:, d_ff:]), ffn_conv_w,
                ffn_conv_b[None, :], bf(ffn_w_down), ln2_g[None, :], ln2_b[None, :], alpha, min(256, s), 256)


def kernel(x, w_in, mla_q_norm, mla_w_uq, mla_kv_norm, mla_w_ukv, rwkv_mu, rwkv_w0, rwkv_w2, rwkv_a0, rwkv_a2, rwkv_g2, rwkv_k_k, rwkv_k_a, rwkv_r_k, rwkv_gn_g, rwkv_gn_b, w_o, ln1_g, ln1_b, ffn_w_up, ffn_conv_w, ffn_conv_b, ffn_w_down, ln2_g, ln2_b):
    depth = w_in.shape[0]
    alpha = (2 * depth) ** 0.25
    rope = _rope_tables(x.shape[1])
    weights = (w_in, mla_q_norm, mla_w_uq, mla_kv_norm, mla_w_ukv, rwkv_mu, rwkv_w0, rwkv_w2, rwkv_a0, rwkv_a2,
               rwkv_g2, rwkv_k_k, rwkv_k_a, rwkv_r_k, rwkv_gn_g, rwkv_gn_b, w_o, ln1_g, ln1_b,
               ffn_w_up, ffn_conv_w, ffn_conv_b, ffn_w_down, ln2_g, ln2_b)
    for layer in range(depth):
        x = _layer(x, *(w[layer] for w in weights), alpha=alpha, rope=rope)
    return x
```

```python
import functools
import math

import jax
import jax.numpy as jnp
from jax import lax
from jax.experimental import pallas as pl
from jax.experimental.pallas import tpu as pltpu

F32 = jnp.float32
BF16 = jnp.bfloat16

HEAD_DIM = 64
SB_HEADS = 4
MLA_HEADS = 4
RWKV_HEADS = 8
SB_WIDTH = SB_HEADS * HEAD_DIM
MLA_WIDTH = MLA_HEADS * HEAD_DIM
RWKV_WIDTH = RWKV_HEADS * HEAD_DIM
MLA_Q_LORA = 192
MLA_KV_LORA = 128
MLA_NOPE_DIM = 64
MLA_ROPE_DIM = 32
MLA_PAD_DIM = 128
ROPE_THETA = 10000.0
RWKV_DECAY_LORA = 64
RWKV_AAA_LORA = 64
RWKV_GATE_LORA = 128
RWKV_GN_EPS = 64e-5
SB_COLS = 3 * SB_WIDTH
MLA_COLS = MLA_Q_LORA + MLA_KV_LORA + MLA_ROPE_DIM
RWKV_COLS = 3 * RWKV_WIDTH + RWKV_DECAY_LORA + RWKV_AAA_LORA + RWKV_GATE_LORA
CONV_WIDTH = 3
LN_EPS = 1e-5
RMS_EPS = 1e-6

LANES = 128
SUBLANES = 8
VMEM_LIMIT = 48 * 1024 * 1024


def _dot(a, b):
    return jnp.dot(a, b, preferred_element_type=F32)


def _dot_nt(a, b):
    return lax.dot_general(a, b, (((1,), (1,)), ((), ())), preferred_element_type=F32)


def _split_bf16(x, terms):
    out = []
    rem = x
    for _ in range(terms):
        part = rem.astype(BF16)
        out.append(part)
        rem = rem - part.astype(F32)
    return out


def _params(sem):
    return pltpu.CompilerParams(dimension_semantics=sem, vmem_limit_bytes=VMEM_LIMIT)


def _inproj_kernel(x_ref, wsb_ref, wcq_ref, wckv_ref, wkr_ref, wrw_ref,
                   q_ref, k_ref, v_ref, cq_ref, ckv_ref, kr_ref, hr_ref):
    xb = x_ref[...].astype(BF16)
    sb = _dot(xb, wsb_ref[...])
    q_ref[...] = sb[:, :SB_WIDTH].astype(BF16)
    k_ref[...] = sb[:, SB_WIDTH:2 * SB_WIDTH].astype(BF16)
    v_ref[...] = sb[:, 2 * SB_WIDTH:].astype(BF16)
    cq_ref[...] = _dot(xb, wcq_ref[...])
    ckv_ref[...] = _dot(xb, wckv_ref[...])
    kr_ref[...] = _dot(xb, wkr_ref[...])
    hr_ref[...] = _dot(xb, wrw_ref[...])


def _inproj(x2d, wsb, wcq, wckv, wkr, wrw, tm):
    t, d = x2d.shape
    row = lambda n: pl.BlockSpec((tm, n), lambda i: (i, 0))
    full = lambda w: pl.BlockSpec(w.shape, lambda i: (0, 0))
    widths = (SB_WIDTH, SB_WIDTH, SB_WIDTH, MLA_Q_LORA, MLA_KV_LORA, 2 * MLA_ROPE_DIM, RWKV_COLS)
    dtypes = (BF16, BF16, BF16, F32, F32, F32, F32)
    return pl.pallas_call(
        _inproj_kernel,
        grid=(t // tm,),
        in_specs=[row(d), full(wsb), full(wcq), full(wckv), full(wkr), full(wrw)],
        out_specs=[row(n) for n in widths],
        out_shape=[jax.ShapeDtypeStruct((t, n), dt) for n, dt in zip(widths, dtypes)],
        compiler_params=_params(("parallel",)),
        name="inproj",
    )(x2d, wsb, wcq, wckv, wkr, wrw)


def _sb_attn_kernel(q_ref, k_ref, v_ref, o_ref, *, bq, bk, scale):
    i = pl.program_id(2)
    ratio = bk // bq
    q = q_ref[0]
    lane = lax.broadcasted_iota(jnp.int32, (bq, LANES), 1)
    urow = lax.broadcasted_iota(jnp.int32, (bk, bk), 0)
    ucol = lax.broadcasted_iota(jnp.int32, (bk, bk), 1)
    upper = jnp.where(urow > ucol, 1.0, 0.0).astype(BF16)
    row = lax.broadcasted_iota(jnp.int32, (bq, bk), 0) + (i % ratio) * bq
    col = lax.broadcasted_iota(jnp.int32, (bq, bk), 1)
    past_diag = col < row
    head_masks = [lane < HEAD_DIM, lane >= HEAD_DIM]
    assert math.frexp(scale)[0] == 0.5
    neg_q = q * jnp.asarray(-scale, q.dtype)
    qh = [jnp.where(m, neg_q, jnp.zeros_like(q)) for m in head_masks]

    def block(j, carry, diag):
        start = pl.multiple_of(j * bk, bk)
        kb = k_ref[0, pl.ds(start, bk), :]
        vb = v_ref[0, pl.ds(start, bk), :]
        new = []
        for h in range(2):
            acc, run = carry[2 * h], carry[2 * h + 1]
            u = _dot_nt(qh[h], kb)
            soft = jnp.log(1.0 + jnp.exp(-jnp.abs(u)))
            log_keep = jnp.minimum(u, 0.0) - soft
            if diag:
                log_keep = jnp.where(past_diag, log_keep, 0.0)
            hi, lo = _split_bf16(log_keep, 2)
            later = _dot(hi, upper) + _dot(lo, upper)
            logw = (log_keep - u) + later + run
            w = jnp.exp(logw)
            if diag:
                w = jnp.where(past_diag, w, 0.0)
            acc = acc + _dot(w.astype(BF16), vb)
            run = run + jnp.sum(log_keep, axis=1, keepdims=True)
            new += [acc, run]
        return tuple(new)

    zero_acc = jnp.zeros((bq, LANES), F32)
    zero_run = jnp.zeros((bq, 1), F32)
    jd = i // ratio
    carry = block(jd, (zero_acc, zero_run, zero_acc, zero_run), True)

    def two_blocks(s, c):
        j = jd - 1 - 2 * s
        return block(j - 1, block(j, c, False), False)

    carry = lax.fori_loop(0, jd // 2, two_blocks, carry)
    carry = lax.cond(jd % 2 == 1, lambda c: block(0, c, False), lambda c: c, carry)
    o_ref[0] = jnp.where(head_masks[0], carry[0], carry[2]).astype(o_ref.dtype)


def _sb_attention(q, k, v, bq, bk):
    b, s, w = q.shape
    pairs = w // LANES
    kern = functools.partial(_sb_attn_kernel, bq=bq, bk=bk, scale=HEAD_DIM ** -0.5)
    return pl.pallas_call(
        kern,
        grid=(b, pairs, s // bq),
        in_specs=[pl.BlockSpec((1, bq, LANES), lambda bi, p, i: (bi, i, p)),
                  pl.BlockSpec((1, s, LANES), lambda bi, p, i: (bi, 0, p)),
                  pl.BlockSpec((1, s, LANES), lambda bi, p, i: (bi, 0, p))],
        out_specs=pl.BlockSpec((1, bq, LANES), lambda bi, p, i: (bi, i, p)),
        out_shape=jax.ShapeDtypeStruct((b, s, w), BF16),
        compiler_params=_params(("parallel", "parallel", "arbitrary")),
        name="sb_attn",
    )(q, k, v)


def _mla_prep_kernel(cq_ref, ckv_ref, kr_ref, qn_ref, kvn_ref, wq_ref, wqr_ref, wkn_ref, wv_ref, scat_ref,
                     qcos_ref, qsin_ref, ktab_ref, q_out, k_out, v_out):
    cq = cq_ref[...]
    cq = cq * lax.rsqrt(jnp.mean(jnp.square(cq), axis=-1, keepdims=True) + RMS_EPS) * qn_ref[...]
    cqb = cq.astype(BF16)
    q = _dot(cqb, wq_ref[...]) * qcos_ref[...] + _dot(cqb, wqr_ref[...]) * qsin_ref[...]
    q_out[...] = q.astype(BF16)
    ckv = ckv_ref[...]
    ckv = ckv * lax.rsqrt(jnp.mean(jnp.square(ckv), axis=-1, keepdims=True) + RMS_EPS) * kvn_ref[...]
    ckvb = ckv.astype(BF16)
    v_out[...] = _dot(ckvb, wv_ref[...]).astype(BF16)
    prod = kr_ref[...] * ktab_ref[...]
    hi, lo = _split_bf16(prod, 2)
    rope = _dot(hi, scat_ref[...]) + _dot(lo, scat_ref[...])
    k_out[...] = (_dot(ckvb, wkn_ref[...]) + rope).astype(BF16)


def _mla_prep(cq, ckv, kr, qn, kvn, wq, wqr, wkn, wv, scat, qcos, qsin, ktab, s, tm):
    t = cq.shape[0]
    nj = s // tm
    row = lambda n: pl.BlockSpec((tm, n), lambda i: (i, 0))
    full = lambda w: pl.BlockSpec(w.shape, lambda i: (0, 0))
    pos = lambda n: pl.BlockSpec((tm, n), lambda i: (i % nj, 0))
    qk_w = MLA_HEADS * MLA_PAD_DIM
    return pl.pallas_call(
        _mla_prep_kernel,
        grid=(t // tm,),
        in_specs=[row(MLA_Q_LORA), row(MLA_KV_LORA), row(2 * MLA_ROPE_DIM), full(qn), full(kvn), full(wq),
                  full(wqr), full(wkn), full(wv), full(scat), pos(qk_w), pos(qk_w), pos(2 * MLA_ROPE_DIM)],
        out_specs=[row(qk_w), row(qk_w), row(MLA_WIDTH)],
        out_shape=[jax.ShapeDtypeStruct((t, qk_w), BF16), jax.ShapeDtypeStruct((t, qk_w), BF16),
                   jax.ShapeDtypeStruct((t, MLA_WIDTH), BF16)],
        compiler_params=_params(("parallel",)),
        name="mla_prep",
    )(cq, ckv, kr, qn, kvn, wq, wqr, wkn, wv, scat, qcos, qsin, ktab)


def _mla_attn_kernel(q_ref, k_ref, v_ref, o_ref, *, blk, scale):
    i = pl.program_id(2)
    q = q_ref[0]
    qh = [q[:, :MLA_PAD_DIM], q[:, MLA_PAD_DIM:]]
    row = lax.broadcasted_iota(jnp.int32, (blk, blk), 0)
    col = lax.broadcasted_iota(jnp.int32, (blk, blk), 1)
    causal_diag = col <= row
    lane = lax.broadcasted_iota(jnp.int32, (blk, LANES), 1)

    def block(j, carry, diag):
        start = pl.multiple_of(j * blk, blk)
        kb = k_ref[0, pl.ds(start, blk), :]
        vb = v_ref[0, pl.ds(start, blk), :]
        new = []
        for h in range(2):
            m, l, acc = carry[3 * h:3 * h + 3]
            sc = _dot_nt(qh[h], kb[:, h * MLA_PAD_DIM:(h + 1) * MLA_PAD_DIM])
            if diag:
                sc = jnp.where(causal_diag, sc, -jnp.inf)
            m_new = jnp.maximum(m, jnp.max(sc, axis=1, keepdims=True))
            alpha = jnp.exp2((m - m_new) * exp2_scale)
            p = jnp.exp2((sc - m_new) * exp2_scale)
            l = alpha * l + jnp.sum(p, axis=1, keepdims=True)
            acc = alpha * acc + _dot(p.astype(BF16), vb)
            new += [m_new, l, acc]
        return tuple(new)

    exp2_scale = scale * math.log2(math.e)
    init = (jnp.full((blk, 1), -jnp.inf, F32), jnp.zeros((blk, 1), F32), jnp.zeros((blk, LANES), F32)) * 2
    carry = lax.fori_loop(0, i // 2, lambda s, c: block(2 * s + 1, block(2 * s, c, False), False), init)
    carry = lax.cond(i % 2 == 1, lambda c: block(i - 1, c, False), lambda c: c, carry)
    carry = block(i, carry, True)
    out = jnp.where(lane < HEAD_DIM, carry[2] / carry[1], carry[5] / carry[4])
    o_ref[0] = out.astype(o_ref.dtype)


def _mla_attention(q, k, v, blk):
    b, s, _ = q.shape
    pairs = MLA_HEADS // 2
    kern = functools.partial(_mla_attn_kernel, blk=blk, scale=(MLA_NOPE_DIM + MLA_ROPE_DIM) ** -0.5)
    return pl.pallas_call(
        kern,
        grid=(b, pairs, s // blk),
        in_specs=[pl.BlockSpec((1, blk, 2 * MLA_PAD_DIM), lambda bi, p, i: (bi, i, p)),
                  pl.BlockSpec((1, s, 2 * MLA_PAD_DIM), lambda bi, p, i: (bi, 0, p)),
                  pl.BlockSpec((1, s, LANES), lambda bi, p, i: (bi, 0, p))],
        out_specs=pl.BlockSpec((1, blk, LANES), lambda bi, p, i: (bi, i, p)),
        out_shape=jax.ShapeDtypeStruct((b, s, MLA_WIDTH), BF16),
        compiler_params=_params(("parallel", "parallel", "arbitrary")),
        name="mla_attn",
    )(q, k, v)


def _rwkv_prep_kernel(h_ref, mu_ref, w0_ref, a0_ref, wlora_ref, g2_ref,
                      r_out, w_out, k_out, v_out, a_out, g_out, shift_ref):
    tm = h_ref.shape[1]
    c = RWKV_WIDTH

    @pl.when(pl.program_id(1) == 0)
    def _():
        shift_ref[0:SUBLANES, :] = jnp.zeros((SUBLANES, RWKV_COLS), F32)

    h = h_ref[0]
    shift_ref[SUBLANES:, :] = h
    prev = shift_ref[pl.ds(SUBLANES - 1, tm), :]
    shift_ref[0:SUBLANES, :] = h[tm - SUBLANES:, :]
    p = h + (prev - h) * mu_ref[...]
    r_out[0] = p[:, :c]
    k_out[0] = p[:, c:2 * c]
    v_out[0] = p[:, 2 * c:3 * c]
    lora_in = p[:, 3 * c:3 * c + LANES]
    lane = lax.broadcasted_iota(jnp.int32, lora_in.shape, 1)
    lora_in = jnp.where(lane < RWKV_DECAY_LORA, jnp.tanh(lora_in), lora_in)
    lora = _dot(lora_in.astype(BF16), wlora_ref[...])
    w = -jax.nn.softplus(-(w0_ref[...] + lora[:, :c])) - 0.5
    w_out[0] = jnp.exp(-jnp.exp(w))
    a_out[0] = jax.nn.sigmoid(a0_ref[...] + lora[:, c:])
    gate_in = jax.nn.sigmoid(p[:, 3 * c + LANES:])
    g_out[0] = _dot(gate_in.astype(BF16), g2_ref[...])


def _rwkv_prep(hr, mu, w0, a0, wlora, g2, tm):
    b, s, _ = hr.shape
    full = lambda w: pl.BlockSpec(w.shape, lambda bi, j: (0, 0))
    out = pl.BlockSpec((1, tm, RWKV_WIDTH), lambda bi, j: (bi, j, 0))
    return pl.pallas_call(
        _rwkv_prep_kernel,
        grid=(b, s // tm),
        in_specs=[pl.BlockSpec((1, tm, RWKV_COLS), lambda bi, j: (bi, j, 0)),
                  full(mu), full(w0), full(a0), full(wlora), full(g2)],
        out_specs=[out] * 6,
        out_shape=[jax.ShapeDtypeStruct((b, s, RWKV_WIDTH), F32)] * 6,
        scratch_shapes=[pltpu.VMEM((tm + SUBLANES, RWKV_COLS), F32)],
        compiler_params=_params(("arbitrary", "arbitrary")),
        name="rwkv_prep",
    )(hr, mu, w0, a0, wlora, g2)


def _rwkv_scan_kernel(r_ref, w_ref, k_ref, v_ref, a_ref, kk_ref, ka_ref, rk_ref, gng_ref, gnb_ref,
                      y_ref, state_ref, nkk_ref, b_ref, kp_ref):
    n = HEAD_DIM
    tb = r_ref.shape[0]

    @pl.when(pl.program_id(0) == 0)
    def _():
        state_ref[...] = jnp.zeros(state_ref.shape, F32)

    def step(t, _):
        kt = k_ref[t]
        at = a_ref[t]
        vt = v_ref[t]
        kk = kt * kk_ref[...]
        kk = kk * lax.rsqrt(jnp.maximum(jnp.sum(jnp.square(kk), axis=0, keepdims=True), 1e-12))
        kp = kt * (1.0 + (at - 1.0) * ka_ref[...])
        nkk_ref[...] = -kk
        b_ref[...] = kk * at
        kp_ref[...] = kp
        sa = jnp.zeros((n, y_ref.shape[2]), F32)
        for c in range(n):
            sa = sa + state_ref[c] * nkk_ref[c:c + 1, :]
        y = jnp.zeros_like(sa)
        for c in range(n):
            s_new = (state_ref[c] * w_ref[t, c:c + 1, :] + sa * b_ref[c:c + 1, :]
                     + vt * kp_ref[c:c + 1, :])
            state_ref[c] = s_new
            y = y + s_new * r_ref[t, c:c + 1, :]
        mean = jnp.mean(y, axis=0, keepdims=True)
        var = jnp.mean(jnp.square(y - mean), axis=0, keepdims=True)
        y = (y - mean) * lax.rsqrt(var + RWKV_GN_EPS) * gng_ref[...] + gnb_ref[...]
        bonus = jnp.sum(r_ref[t] * kp * rk_ref[...], axis=0, keepdims=True) * vt
        y_ref[t] = y + bonus
        return 0

    lax.fori_loop(0, tb, step, 0)


def _rwkv_scan(r, w, k, v, a, kk_t, ka_t, rk_t, gng_t, gnb_t, tb):
    s, n, lanes = r.shape
    seq = pl.BlockSpec((tb, n, lanes), lambda i: (i, 0, 0))
    par = pl.BlockSpec((n, lanes), lambda i: (0, 0))
    return pl.pallas_call(
        _rwkv_scan_kernel,
        grid=(s // tb,),
        in_specs=[seq] * 5 + [par] * 5,
        out_specs=seq,
        out_shape=jax.ShapeDtypeStruct((s, n, lanes), F32),
        scratch_shapes=[pltpu.VMEM((n, n, lanes), F32)] + [pltpu.VMEM((n, lanes), F32)] * 3,
        compiler_params=_params(("arbitrary",)),
        name="rwkv_scan",
    )(r, w, k, v, a, kk_t, ka_t, rk_t, gng_t, gnb_t)


def _layer_norm(x, g, b):
    mu = jnp.mean(x, axis=-1, keepdims=True)
    xc = x - mu
    var = jnp.mean(jnp.square(xc), axis=-1, keepdims=True)
    return xc * lax.rsqrt(var + LN_EPS) * g + b


def _outproj_kernel(x_ref, sb_ref, mla_ref, y_ref, g_ref, wsb_ref, wmla_ref, wrw_ref, lng_ref, lnb_ref,
                    o_ref, *, alpha):
    mix = _dot(sb_ref[...], wsb_ref[...]) + _dot(mla_ref[...], wmla_ref[...])
    mix = mix + _dot((y_ref[...] * g_ref[...]).astype(BF16), wrw_ref[...])
    o_ref[...] = _layer_norm(alpha * x_ref[...] + mix, lng_ref[...], lnb_ref[...])


def _outproj(x2d, o_sb, o_mla, y, g, wsb, wmla, wrw, lng, lnb, alpha, tm):
    t, d = x2d.shape
    row = lambda n: pl.BlockSpec((tm, n), lambda i: (i, 0))
    full = lambda w: pl.BlockSpec(w.shape, lambda i: (0, 0))
    return pl.pallas_call(
        functools.partial(_outproj_kernel, alpha=alpha),
        grid=(t // tm,),
        in_specs=[row(d), row(SB_WIDTH), row(MLA_WIDTH), row(RWKV_WIDTH), row(RWKV_WIDTH),
                  full(wsb), full(wmla), full(wrw), full(lng), full(lnb)],
        out_specs=row(d),
        out_shape=jax.ShapeDtypeStruct((t, d), F32),
        compiler_params=_params(("parallel",)),
        name="outproj_ln",
    )(x2d, o_sb, o_mla, y, g, wsb, wmla, wrw, lng, lnb)


def _ffn_kernel(x_ref, wact_ref, wgate_ref, convw_ref, convb_ref, wdown_ref, lng_ref, lnb_ref,
                o_ref, hid_ref, shift_ref, carry_ref, *, alpha, fc):
    tm = x_ref.shape[1]
    d_ff = wact_ref.shape[1]

    @pl.when(pl.program_id(1) == 0)
    def _():
        carry_ref[...] = jnp.zeros(carry_ref.shape, F32)

    x = x_ref[0]
    xb = x.astype(BF16)
    for c in range(d_ff // fc):
        cols = slice(c * fc, (c + 1) * fc)
        u = _dot(xb, wact_ref[:, cols])
        shift_ref[0:SUBLANES, :] = carry_ref[:, cols]
        shift_ref[SUBLANES:, :] = u
        carry_ref[:, cols] = u[tm - SUBLANES:, :]
        prev1 = shift_ref[pl.ds(SUBLANES - 1, tm), :]
        prev2 = shift_ref[pl.ds(SUBLANES - 2, tm), :]
        conv = (prev2 * convw_ref[0:1, cols] + prev1 * convw_ref[1:2, cols] + u * convw_ref[2:3, cols]
                + convb_ref[:, cols])
        gate = _dot(xb, wgate_ref[:, cols])
        gelu = 0.5 * conv * (1.0 + lax.erf(conv * math.sqrt(0.5)))
        hid_ref[:, cols] = (gelu * gate).astype(BF16)
    down = _dot(hid_ref[...], wdown_ref[...])
    o_ref[0] = _layer_norm(alpha * x + down, lng_ref[...], lnb_ref[...])


def _ffn(x, wact, wgate, convw, convb, wdown, lng, lnb, alpha, tm, fc):
    b, s, d = x.shape
    d_ff = wact.shape[1]
    const = lambda w: pl.BlockSpec(w.shape, lambda bi, j: (0, 0), pipeline_mode=pl.Buffered(1))
    tile = pl.BlockSpec((1, tm, d), lambda bi, j: (bi, j, 0))
    return pl.pallas_call(
        functools.partial(_ffn_kernel, alpha=alpha, fc=fc),
        grid=(b, s // tm),
        in_specs=[tile, const(wact), const(wgate), const(convw), const(convb), const(wdown),
                  const(lng), const(lnb)],
        out_specs=tile,
        out_shape=jax.ShapeDtypeStruct((b, s, d), F32),
        scratch_shapes=[pltpu.VMEM((tm, d_ff), BF16),
                        pltpu.VMEM((tm + SUBLANES, fc), F32),
                        pltpu.VMEM((SUBLANES, d_ff), F32)],
        compiler_params=_params(("arbitrary", "arbitrary")),
        name="conv_ffn",
    )(x, wact, wgate, convw, convb, wdown, lng, lnb)


def _rope_tables(s):
    half = MLA_ROPE_DIM // 2
    inv_freq = 1.0 / (ROPE_THETA ** (jnp.arange(0, MLA_ROPE_DIM, 2, dtype=F32) / MLA_ROPE_DIM))
    ang = jnp.arange(s, dtype=F32)[:, None] * inv_freq[None, :]
    cos, sin = jnp.cos(ang), jnp.sin(ang)
    cos2 = jnp.concatenate([cos, cos], axis=-1)
    sin2 = jnp.concatenate([-sin, sin], axis=-1)
    pad = jnp.zeros((s, MLA_PAD_DIM - MLA_NOPE_DIM - MLA_ROPE_DIM), F32)
    qcos = jnp.tile(jnp.concatenate([jnp.ones((s, MLA_NOPE_DIM), F32), cos2, pad], axis=-1), (1, MLA_HEADS))
    qsin = jnp.tile(jnp.concatenate([jnp.zeros((s, MLA_NOPE_DIM), F32), sin2, pad], axis=-1), (1, MLA_HEADS))
    ktab = jnp.concatenate([cos2, sin2], axis=-1)
    del half
    return qcos, qsin, ktab


def _swap_halves(w):
    half = w.shape[-1] // 2
    return jnp.concatenate([w[..., half:], w[..., :half]], axis=-1)


def _mla_weights(w_uq, w_ukv):
    qd = MLA_NOPE_DIM + MLA_ROPE_DIM
    pad = MLA_PAD_DIM - qd
    uq = w_uq.reshape(MLA_Q_LORA, MLA_HEADS, qd)
    zq = jnp.zeros((MLA_Q_LORA, MLA_HEADS, pad), F32)
    wq = jnp.concatenate([uq, zq], axis=-1)
    wqr = jnp.concatenate([jnp.zeros((MLA_Q_LORA, MLA_HEADS, MLA_NOPE_DIM), F32),
                           _swap_halves(uq[..., MLA_NOPE_DIM:]), zq], axis=-1)
    ukv = w_ukv.reshape(MLA_KV_LORA, MLA_HEADS, MLA_NOPE_DIM + HEAD_DIM)
    wkn = jnp.concatenate([ukv[..., :MLA_NOPE_DIM],
                           jnp.zeros((MLA_KV_LORA, MLA_HEADS, MLA_PAD_DIM - MLA_NOPE_DIM), F32)], axis=-1)
    wv = ukv[..., MLA_NOPE_DIM:]
    eye = jnp.eye(MLA_ROPE_DIM, dtype=F32)[:, None, :]
    scat = jnp.concatenate([jnp.zeros((MLA_ROPE_DIM, MLA_HEADS, MLA_NOPE_DIM), F32),
                            jnp.broadcast_to(eye, (MLA_ROPE_DIM, MLA_HEADS, MLA_ROPE_DIM)),
                            jnp.zeros((MLA_ROPE_DIM, MLA_HEADS, pad), F32)], axis=-1)
    scat = jnp.concatenate([scat, scat], axis=0)
    flat = lambda w: w.reshape(w.shape[0], -1).astype(BF16)
    return flat(wq), flat(wqr), flat(wkn), flat(wv), flat(scat)


def _to_chain_layout(t, b, s):
    return t.reshape(b, s, RWKV_HEADS, HEAD_DIM).transpose(1, 3, 0, 2).reshape(s, HEAD_DIM, b * RWKV_HEADS)


def _from_chain_layout(t, b, s):
    return t.reshape(s, HEAD_DIM, b, RWKV_HEADS).transpose(2, 0, 3, 1).reshape(b, s, RWKV_WIDTH)


def _chain_param(p, b):
    return jnp.tile(p.reshape(RWKV_HEADS, HEAD_DIM).T, (1, b))


def _layer(x, w_in, mla_q_norm, mla_w_uq, mla_kv_norm, mla_w_ukv, rwkv_mu, rwkv_w0, rwkv_w2, rwkv_a0, rwkv_a2,
           rwkv_g2, rwkv_k_k, rwkv_k_a, rwkv_r_k, rwkv_gn_g, rwkv_gn_b, w_o, ln1_g, ln1_b,
           ffn_w_up, ffn_conv_w, ffn_conv_b, ffn_w_down, ln2_g, ln2_b, *, alpha, rope):
    b, s, d = x.shape
    t = b * s
    x2d = x.reshape(t, d)
    row_tile = min(512, s)
    attn_blk = min(256, s)

    c0 = SB_COLS
    c1 = c0 + MLA_Q_LORA
    c2 = c1 + MLA_KV_LORA
    c3 = c2 + MLA_ROPE_DIM
    w_kr = w_in[:, c2:c3]
    wkr2 = jnp.concatenate([w_kr, _swap_halves(w_kr)], axis=-1)
    bf = lambda w: w.astype(BF16)
    q_sb, k_sb, v_sb, cq, ckv, kr, hr = _inproj(
        x2d, bf(w_in[:, :c0]), bf(w_in[:, c0:c1]), bf(w_in[:, c1:c2]), bf(wkr2), bf(w_in[:, c3:]), row_tile)

    o_sb = _sb_attention(q_sb.reshape(b, s, -1), k_sb.reshape(b, s, -1), v_sb.reshape(b, s, -1),
                         attn_blk, attn_blk)

    wq, wqr, wkn, wv, scat = _mla_weights(mla_w_uq, mla_w_ukv)
    qcos, qsin, ktab = rope
    q_m, k_m, v_m = _mla_prep(cq, ckv, kr, mla_q_norm[None, :], mla_kv_norm[None, :], wq, wqr, wkn, wv, scat,
                              qcos, qsin, ktab, s, row_tile)
    o_mla = _mla_attention(q_m.reshape(b, s, -1), k_m.reshape(b, s, -1), v_m.reshape(b, s, -1), attn_blk)

    zl = jnp.zeros((RWKV_DECAY_LORA, RWKV_WIDTH), F32)
    wlora = jnp.concatenate([jnp.concatenate([rwkv_w2, zl], axis=1),
                             jnp.concatenate([zl, rwkv_a2], axis=1)], axis=0)
    r, w, k, v, a, g = _rwkv_prep(hr.reshape(b, s, -1), rwkv_mu[None, :], rwkv_w0[None, :], rwkv_a0[None, :],
                                  bf(wlora), bf(rwkv_g2), row_tile)
    chain = lambda z: _to_chain_layout(z, b, s)
    y = _rwkv_scan(chain(r), chain(w), chain(k), chain(v), chain(a),
                   _chain_param(rwkv_k_k, b), _chain_param(rwkv_k_a, b), _chain_param(rwkv_r_k.reshape(-1), b),
                   _chain_param(rwkv_gn_g, b), _chain_param(rwkv_gn_b, b), min(32, s))
    y = _from_chain_layout(y, b, s)

    x1 = _outproj(x2d, o_sb.reshape(t, -1), o_mla.reshape(t, -1), y.reshape(t, -1), g.reshape(t, -1),
                  bf(w_o[:SB_WIDTH]), bf(w_o[SB_WIDTH:SB_WIDTH + MLA_WIDTH]), bf(w_o[SB_WIDTH + MLA_WIDTH:]),
                  ln1_g[None, :], ln1_b[None, :], alpha, row_tile)

    d_ff = ffn_w_down.shape[0]
    return _ffn(x1.reshape(b, s, d), bf(ffn_w_up[:, :d_ff]), bf(ffn_w_up[:, d_ff:]), ffn_conv_w,
                ffn_conv_b[None, :], bf(ffn_w_down), ln2_g[None, :], ln2_b[None, :], alpha, min(256, s), 256)


def kernel(x, w_in, mla_q_norm, mla_w_uq, mla_kv_norm, mla_w_ukv, rwkv_mu, rwkv_w0, rwkv_w2, rwkv_a0, rwkv_a2, rwkv_g2, rwkv_k_k, rwkv_k_a, rwkv_r_k, rwkv_gn_g, rwkv_gn_b, w_o, ln1_g, ln1_b, ffn_w_up, ffn_conv_w, ffn_conv_b, ffn_w_down, ln2_g, ln2_b):
    depth = w_in.shape[0]
    alpha = (2 * depth) ** 0.25
    rope = _rope_tables(x.shape[1])
    weights = (w_in, mla_q_norm, mla_w_uq, mla_kv_norm, mla_w_ukv, rwkv_mu, rwkv_w0, rwkv_w2, rwkv_a0, rwkv_a2,
               rwkv_g2, rwkv_k_k, rwkv_k_a, rwkv_r_k, rwkv_gn_g, rwkv_gn_b, w_o, ln1_g, ln1_b,
               ffn_w_up, ffn_conv_w, ffn_conv_b, ffn_w_down, ln2_g, ln2_b)
    for layer in range(depth):
        x = _layer(x, *(w[layer] for w in weights), alpha=alpha, rope=rope)
    return x
```

```python
import functools
import math

import jax
import jax.numpy as jnp
from jax import lax
from jax.experimental import pallas as pl
from jax.experimental.pallas import tpu as pltpu

F32 = jnp.float32
BF16 = jnp.bfloat16

HEAD_DIM = 64
SB_HEADS = 4
MLA_HEADS = 4
RWKV_HEADS = 8
SB_WIDTH = SB_HEADS * HEAD_DIM
MLA_WIDTH = MLA_HEADS * HEAD_DIM
RWKV_WIDTH = RWKV_HEADS * HEAD_DIM
MLA_Q_LORA = 192
MLA_KV_LORA = 128
MLA_NOPE_DIM = 64
MLA_ROPE_DIM = 32
MLA_PAD_DIM = 128
ROPE_THETA = 10000.0
RWKV_DECAY_LORA = 64
RWKV_AAA_LORA = 64
RWKV_GATE_LORA = 128
RWKV_GN_EPS = 64e-5
RWKV_CHUNK = 64
SB_COLS = 3 * SB_WIDTH
MLA_COLS = MLA_Q_LORA + MLA_KV_LORA + MLA_ROPE_DIM
RWKV_COLS = 3 * RWKV_WIDTH + RWKV_DECAY_LORA + RWKV_AAA_LORA + RWKV_GATE_LORA
CONV_WIDTH = 3
LN_EPS = 1e-5
RMS_EPS = 1e-6

LANES = 128
SUBLANES = 8
VMEM_LIMIT = 48 * 1024 * 1024


def _dot(a, b):
    return jnp.dot(a, b, preferred_element_type=F32)


def _dot_nt(a, b):
    return lax.dot_general(a, b, (((1,), (1,)), ((), ())), preferred_element_type=F32)


def _dot_tn(a, b):
    return lax.dot_general(a, b, (((0,), (0,)), ((), ())), preferred_element_type=F32)


def _split_bf16(x, terms):
    out = []
    rem = x
    for _ in range(terms):
        part = rem.astype(BF16)
        out.append(part)
        rem = rem - part.astype(F32)
    return out


def _params(sem):
    return pltpu.CompilerParams(dimension_semantics=sem, vmem_limit_bytes=VMEM_LIMIT)


def _inproj_kernel(x_ref, wsb_ref, wcq_ref, wckv_ref, wkr_ref, wrw_ref,
                   q_ref, k_ref, v_ref, cq_ref, ckv_ref, kr_ref, hr_ref):
    xb = x_ref[...].astype(BF16)
    sb = _dot(xb, wsb_ref[...])
    q_ref[...] = sb[:, :SB_WIDTH].astype(BF16)
    k_ref[...] = sb[:, SB_WIDTH:2 * SB_WIDTH].astype(BF16)
    v_ref[...] = sb[:, 2 * SB_WIDTH:].astype(BF16)
    cq_ref[...] = _dot(xb, wcq_ref[...])
    ckv_ref[...] = _dot(xb, wckv_ref[...])
    kr_ref[...] = _dot(xb, wkr_ref[...])
    hr_ref[...] = _dot(xb, wrw_ref[...])


def _inproj(x2d, wsb, wcq, wckv, wkr, wrw, tm):
    t, d = x2d.shape
    row = lambda n: pl.BlockSpec((tm, n), lambda i: (i, 0))
    full = lambda w: pl.BlockSpec(w.shape, lambda i: (0, 0))
    widths = (SB_WIDTH, SB_WIDTH, SB_WIDTH, MLA_Q_LORA, MLA_KV_LORA, 2 * MLA_ROPE_DIM, RWKV_COLS)
    dtypes = (BF16, BF16, BF16, F32, F32, F32, F32)
    return pl.pallas_call(
        _inproj_kernel,
        grid=(t // tm,),
        in_specs=[row(d), full(wsb), full(wcq), full(wckv), full(wkr), full(wrw)],
        out_specs=[row(n) for n in widths],
        out_shape=[jax.ShapeDtypeStruct((t, n), dt) for n, dt in zip(widths, dtypes)],
        compiler_params=_params(("parallel",)),
        name="inproj",
    )(x2d, wsb, wcq, wckv, wkr, wrw)


def _sb_attn_kernel(q_ref, k_ref, v_ref, o_ref, *, bq, bk, scale):
    i = pl.program_id(2)
    ratio = bk // bq
    q = q_ref[0]
    lane = lax.broadcasted_iota(jnp.int32, (bq, LANES), 1)
    urow = lax.broadcasted_iota(jnp.int32, (bk, bk), 0)
    ucol = lax.broadcasted_iota(jnp.int32, (bk, bk), 1)
    upper = jnp.where(urow > ucol, 1.0, 0.0).astype(BF16)
    row = lax.broadcasted_iota(jnp.int32, (bq, bk), 0) + (i % ratio) * bq
    col = lax.broadcasted_iota(jnp.int32, (bq, bk), 1)
    past_diag = col < row
    head_masks = [lane < HEAD_DIM, lane >= HEAD_DIM]
    assert math.frexp(scale)[0] == 0.5
    neg_q = q * jnp.asarray(-scale, q.dtype)
    qh = [jnp.where(m, neg_q, jnp.zeros_like(q)) for m in head_masks]

    def block(j, carry, diag):
        start = pl.multiple_of(j * bk, bk)
        kb = k_ref[0, pl.ds(start, bk), :]
        vb = v_ref[0, pl.ds(start, bk), :]
        new = []
        for h in range(2):
            acc, run = carry[2 * h], carry[2 * h + 1]
            u = _dot_nt(qh[h], kb)
            soft = jnp.log(1.0 + jnp.exp(-jnp.abs(u)))
            log_keep = jnp.minimum(u, 0.0) - soft
            if diag:
                log_keep = jnp.where(past_diag, log_keep, 0.0)
            hi, lo = _split_bf16(log_keep, 2)
            later = _dot(hi, upper) + _dot(lo, upper)
            logw = (log_keep - u) + later + run
            w = jnp.exp(logw)
            if diag:
                w = jnp.where(past_diag, w, 0.0)
            acc = acc + _dot(w.astype(BF16), vb)
            run = run + jnp.sum(log_keep, axis=1, keepdims=True)
            new += [acc, run]
        return tuple(new)

    zero_acc = jnp.zeros((bq, LANES), F32)
    zero_run = jnp.zeros((bq, 1), F32)
    jd = i // ratio
    carry = block(jd, (zero_acc, zero_run, zero_acc, zero_run), True)

    def two_blocks(s, c):
        j = jd - 1 - 2 * s
        return block(j - 1, block(j, c, False), False)

    carry = lax.fori_loop(0, jd // 2, two_blocks, carry)
    carry = lax.cond(jd % 2 == 1, lambda c: block(0, c, False), lambda c: c, carry)
    o_ref[0] = jnp.where(head_masks[0], carry[0], carry[2]).astype(o_ref.dtype)


def _sb_attention(q, k, v, bq, bk):
    b, s, w = q.shape
    pairs = w // LANES
    kern = functools.partial(_sb_attn_kernel, bq=bq, bk=bk, scale=HEAD_DIM ** -0.5)
    return pl.pallas_call(
        kern,
        grid=(b, pairs, s // bq),
        in_specs=[pl.BlockSpec((1, bq, LANES), lambda bi, p, i: (bi, i, p)),
                  pl.BlockSpec((1, s, LANES), lambda bi, p, i: (bi, 0, p)),
                  pl.BlockSpec((1, s, LANES), lambda bi, p, i: (bi, 0, p))],
        out_specs=pl.BlockSpec((1, bq, LANES), lambda bi, p, i: (bi, i, p)),
        out_shape=jax.ShapeDtypeStruct((b, s, w), BF16),
        compiler_params=_params(("parallel", "parallel", "arbitrary")),
        name="sb_attn",
    )(q, k, v)


def _mla_prep_kernel(cq_ref, ckv_ref, kr_ref, qn_ref, kvn_ref, wq_ref, wqr_ref, wkn_ref, wv_ref, scat_ref,
                     qcos_ref, qsin_ref, ktab_ref, q_out, k_out, v_out):
    cq = cq_ref[...]
    cq = cq * lax.rsqrt(jnp.mean(jnp.square(cq), axis=-1, keepdims=True) + RMS_EPS) * qn_ref[...]
    cqb = cq.astype(BF16)
    q = _dot(cqb, wq_ref[...]) * qcos_ref[...] + _dot(cqb, wqr_ref[...]) * qsin_ref[...]
    q_out[...] = q.astype(BF16)
    ckv = ckv_ref[...]
    ckv = ckv * lax.rsqrt(jnp.mean(jnp.square(ckv), axis=-1, keepdims=True) + RMS_EPS) * kvn_ref[...]
    ckvb = ckv.astype(BF16)
    v_out[...] = _dot(ckvb, wv_ref[...]).astype(BF16)
    prod = kr_ref[...] * ktab_ref[...]
    hi, lo = _split_bf16(prod, 2)
    rope = _dot(hi, scat_ref[...]) + _dot(lo, scat_ref[...])
    k_out[...] = (_dot(ckvb, wkn_ref[...]) + rope).astype(BF16)


def _mla_prep(cq, ckv, kr, qn, kvn, wq, wqr, wkn, wv, scat, qcos, qsin, ktab, s, tm):
    t = cq.shape[0]
    nj = s // tm
    row = lambda n: pl.BlockSpec((tm, n), lambda i: (i, 0))
    full = lambda w: pl.BlockSpec(w.shape, lambda i: (0, 0))
    pos = lambda n: pl.BlockSpec((tm, n), lambda i: (i % nj, 0))
    qk_w = MLA_HEADS * MLA_PAD_DIM
    return pl.pallas_call(
        _mla_prep_kernel,
        grid=(t // tm,),
        in_specs=[row(MLA_Q_LORA), row(MLA_KV_LORA), row(2 * MLA_ROPE_DIM), full(qn), full(kvn), full(wq),
                  full(wqr), full(wkn), full(wv), full(scat), pos(qk_w), pos(qk_w), pos(2 * MLA_ROPE_DIM)],
        out_specs=[row(qk_w), row(qk_w), row(MLA_WIDTH)],
        out_shape=[jax.ShapeDtypeStruct((t, qk_w), BF16), jax.ShapeDtypeStruct((t, qk_w), BF16),
                   jax.ShapeDtypeStruct((t, MLA_WIDTH), BF16)],
        compiler_params=_params(("parallel",)),
        name="mla_prep",
    )(cq, ckv, kr, qn, kvn, wq, wqr, wkn, wv, scat, qcos, qsin, ktab)


def _mla_attn_kernel(q_ref, k_ref, v_ref, o_ref, *, blk, scale):
    i = pl.program_id(2)
    q = q_ref[0]
    qh = [q[:, :MLA_PAD_DIM], q[:, MLA_PAD_DIM:]]
    row = lax.broadcasted_iota(jnp.int32, (blk, blk), 0)
    col = lax.broadcasted_iota(jnp.int32, (blk, blk), 1)
    causal_diag = col <= row
    lane = lax.broadcasted_iota(jnp.int32, (blk, LANES), 1)

    def block(j, carry, diag):
        start = pl.multiple_of(j * blk, blk)
        kb = k_ref[0, pl.ds(start, blk), :]
        vb = v_ref[0, pl.ds(start, blk), :]
        new = []
        for h in range(2):
            m, l, acc = carry[3 * h:3 * h + 3]
            sc = _dot_nt(qh[h], kb[:, h * MLA_PAD_DIM:(h + 1) * MLA_PAD_DIM])
            if diag:
                sc = jnp.where(causal_diag, sc, -jnp.inf)
            m_new = jnp.maximum(m, jnp.max(sc, axis=1, keepdims=True))
            alpha = jnp.exp2((m - m_new) * exp2_scale)
            p = jnp.exp2((sc - m_new) * exp2_scale)
            l = alpha * l + jnp.sum(p, axis=1, keepdims=True)
            acc = alpha * acc + _dot(p.astype(BF16), vb)
            new += [m_new, l, acc]
        return tuple(new)

    exp2_scale = scale * math.log2(math.e)
    init = (jnp.full((blk, 1), -jnp.inf, F32), jnp.zeros((blk, 1), F32), jnp.zeros((blk, LANES), F32)) * 2
    carry = lax.fori_loop(0, i // 2, lambda s, c: block(2 * s + 1, block(2 * s, c, False), False), init)
    carry = lax.cond(i % 2 == 1, lambda c: block(i - 1, c, False), lambda c: c, carry)
    carry = block(i, carry, True)
    out = jnp.where(lane < HEAD_DIM, carry[2] / carry[1], carry[5] / carry[4])
    o_ref[0] = out.astype(o_ref.dtype)


def _mla_attention(q, k, v, blk):
    b, s, _ = q.shape
    pairs = MLA_HEADS // 2
    kern = functools.partial(_mla_attn_kernel, blk=blk, scale=(MLA_NOPE_DIM + MLA_ROPE_DIM) ** -0.5)
    return pl.pallas_call(
        kern,
        grid=(b, pairs, s // blk),
        in_specs=[pl.BlockSpec((1, blk, 2 * MLA_PAD_DIM), lambda bi, p, i: (bi, i, p)),
                  pl.BlockSpec((1, s, 2 * MLA_PAD_DIM), lambda bi, p, i: (bi, 0, p)),
                  pl.BlockSpec((1, s, LANES), lambda bi, p, i: (bi, 0, p))],
        out_specs=pl.BlockSpec((1, blk, LANES), lambda bi, p, i: (bi, i, p)),
        out_shape=jax.ShapeDtypeStruct((b, s, MLA_WIDTH), BF16),
        compiler_params=_params(("parallel", "parallel", "arbitrary")),
        name="mla_attn",
    )(q, k, v)


def _rwkv_prep_kernel(h_ref, mu_ref, w0_ref, a0_ref, wlora_ref, g2_ref,
                      r_out, w_out, k_out, v_out, a_out, g_out, shift_ref):
    tm = h_ref.shape[1]
    c = RWKV_WIDTH

    @pl.when(pl.program_id(1) == 0)
    def _():
        shift_ref[0:SUBLANES, :] = jnp.zeros((SUBLANES, RWKV_COLS), F32)

    h = h_ref[0]
    shift_ref[SUBLANES:, :] = h
    prev = shift_ref[pl.ds(SUBLANES - 1, tm), :]
    shift_ref[0:SUBLANES, :] = h[tm - SUBLANES:, :]
    p = h + (prev - h) * mu_ref[...]
    r_out[0] = p[:, :c]
    k_out[0] = p[:, c:2 * c]
    v_out[0] = p[:, 2 * c:3 * c]
    lora_in = p[:, 3 * c:3 * c + LANES]
    lane = lax.broadcasted_iota(jnp.int32, lora_in.shape, 1)
    lora_in = jnp.where(lane < RWKV_DECAY_LORA, jnp.tanh(lora_in), lora_in)
    lora = _dot(lora_in.astype(BF16), wlora_ref[...])
    w = -jax.nn.softplus(-(w0_ref[...] + lora[:, :c])) - 0.5
    w_out[0] = -jnp.exp(w)
    a_out[0] = jax.nn.sigmoid(a0_ref[...] + lora[:, c:])
    gate_in = jax.nn.sigmoid(p[:, 3 * c + LANES:])
    g_out[0] = _dot(gate_in.astype(BF16), g2_ref[...])


def _rwkv_prep(hr, mu, w0, a0, wlora, g2, tm):
    b, s, _ = hr.shape
    full = lambda w: pl.BlockSpec(w.shape, lambda bi, j: (0, 0))
    out = pl.BlockSpec((1, tm, RWKV_WIDTH), lambda bi, j: (bi, j, 0))
    return pl.pallas_call(
        _rwkv_prep_kernel,
        grid=(b, s // tm),
        in_specs=[pl.BlockSpec((1, tm, RWKV_COLS), lambda bi, j: (bi, j, 0)),
                  full(mu), full(w0), full(a0), full(wlora), full(g2)],
        out_specs=[out] * 6,
        out_shape=[jax.ShapeDtypeStruct((b, s, RWKV_WIDTH), F32)] * 6,
        scratch_shapes=[pltpu.VMEM((tm + SUBLANES, RWKV_COLS), F32)],
        compiler_params=_params(("arbitrary", "arbitrary")),
        name="rwkv_prep",
    )(hr, mu, w0, a0, wlora, g2)


def _mm(a, b, dot=_dot):
    return dot(a.astype(BF16), b.astype(BF16))


def _rwkv_chunk_kernel(r_ref, lw_ref, k_ref, v_ref, a_ref, g_ref, kk_ref, ka_ref, rk_ref, gng_ref, gnb_ref,
                       y_ref, state_ref, *, chunk, n_sub):
    c = chunk
    n = HEAD_DIM
    pairs = RWKV_HEADS // 2

    @pl.when(pl.program_id(1) == 0)
    def _():
        state_ref[...] = jnp.zeros(state_ref.shape, F32)

    rowi = lax.broadcasted_iota(jnp.int32, (c, c), 0)
    coli = lax.broadcasted_iota(jnp.int32, (c, c), 1)
    tri_incl = jnp.where(rowi >= coli, 1.0, 0.0).astype(BF16)
    r2 = lax.broadcasted_iota(jnp.int32, (2 * c, 2 * c), 0)
    c2 = lax.broadcasted_iota(jnp.int32, (2 * c, 2 * c), 1)
    same_head = (r2 >= c) == (c2 >= c)
    strict = jnp.logical_and(same_head, r2 > c2)
    incl = jnp.logical_and(same_head, r2 >= c2)
    eye2 = jnp.where(r2 == c2, 1.0, 0.0).astype(F32)
    lane = lax.broadcasted_iota(jnp.int32, (c, LANES), 1)
    m0 = lane < n
    own = jnp.concatenate([m0, jnp.logical_not(m0)], axis=0)
    lr = lax.broadcasted_iota(jnp.int32, (LANES, LANES), 0)
    lc = lax.broadcasted_iota(jnp.int32, (LANES, LANES), 1)
    blockdiag = (lr >= n) == (lc >= n)

    def stack(x):
        return jnp.where(own, jnp.concatenate([x, x], axis=0), 0.0)

    def unstack(x):
        return jnp.where(m0, x[:c], x[c:])

    for sub in range(n_sub):
        rows = pl.ds(sub * c, c)
        r = r_ref[0, rows, :]
        logw = lw_ref[0, rows, :]
        k = k_ref[0, rows, :]
        v = v_ref[0, rows, :]
        a = a_ref[0, rows, :]
        kk = k * kk_ref[...]
        kp = k * (1.0 + (a - 1.0) * ka_ref[...])
        bon = r * kp * rk_ref[...]
        l3 = _split_bf16(logw, 3)
        cum = _dot(tri_incl, l3[0]) + _dot(tri_incl, l3[1]) + _dot(tri_incl, l3[2])
        g_in = jnp.exp(cum)
        g_ex = jnp.exp(cum - logw)
        g_inv = jnp.exp(-cum)
        rt = r * g_in
        kt = kp * g_inv
        ag = a * g_inv
        g_end = g_in[c - 1:c, :]
        P = range(pairs)
        ln = [slice(p * LANES, (p + 1) * LANES) for p in P]
        KK = [stack(kk[:, ln[p]]) for p in P]
        KK = [x * lax.rsqrt(jnp.maximum(jnp.sum(jnp.square(x), axis=1, keepdims=True), 1e-12)) for x in KK]
        A = [-KK[p] * stack(g_ex[:, ln[p]]) for p in P]
        B = [KK[p] * stack(ag[:, ln[p]]) for p in P]
        R = [stack(rt[:, ln[p]]) for p in P]
        K = [stack(kt[:, ln[p]]) for p in P]
        V = [stack(v[:, ln[p]]) for p in P]
        bonus = [jnp.sum(stack(bon[:, ln[p]]), axis=1, keepdims=True) * V[p] for p in P]
        big = [_mm(jnp.concatenate([A[p], R[p]], axis=0), jnp.concatenate([B[p], K[p]], axis=0), _dot_nt)
               for p in P]
        a_ab = [jnp.where(strict, big[p][:2 * c, :2 * c], 0.0) for p in P]
        a_ak = [jnp.where(strict, big[p][:2 * c, 2 * c:], 0.0) for p in P]
        a_rb = [jnp.where(incl, big[p][2 * c:, :2 * c], 0.0) for p in P]
        a_rk = [jnp.where(incl, big[p][2 * c:, 2 * c:], 0.0) for p in P]
        T = [eye2 + a_ab[p] for p in P]
        Lp = [_mm(a_ab[p], a_ab[p]) for p in P]
        for _ in range(int(math.log2(c)) - 2):
            X = [_mm(jnp.concatenate([T[p], Lp[p]], axis=0), Lp[p]) for p in P]
            T = [T[p] + X[p][:2 * c] for p in P]
            Lp = [X[p][2 * c:] for p in P]
        T = [T[p] + _mm(T[p], Lp[p]) for p in P]
        akv = [_mm(a_ak[p], V[p]) for p in P]
        W = [_mm(T[p], jnp.concatenate([A[p], akv[p]], axis=1)) for p in P]
        ys = []
        for p in P:
            H0T = state_ref[p]
            U = _mm(W[p][:, :LANES], H0T, _dot_nt) + W[p][:, LANES:]
            Y = (_mm(R[p], H0T, _dot_nt)
                 + _mm(jnp.concatenate([a_rb[p], a_rk[p]], axis=1), jnp.concatenate([U, V[p]], axis=0)))
            upd = _mm(jnp.concatenate([U, V[p]], axis=0), jnp.concatenate([B[p], K[p]], axis=0), _dot_tn)
            state_ref[p] = (H0T + jnp.where(blockdiag, upd, 0.0)) * g_end[:, ln[p]]
            mean = jnp.sum(Y, axis=1, keepdims=True) * (1.0 / n)
            yc = jnp.where(own, Y - mean, 0.0)
            var = jnp.sum(jnp.square(yc), axis=1, keepdims=True) * (1.0 / n)
            ys.append((unstack(yc * lax.rsqrt(var + RWKV_GN_EPS)), unstack(bonus[p])))
        yn = jnp.concatenate([t[0] for t in ys], axis=1)
        bo = jnp.concatenate([t[1] for t in ys], axis=1)
        y_ref[0, rows, :] = (yn * gng_ref[...] + gnb_ref[...] + bo) * g_ref[0, rows, :]


def _rwkv_chunked(r, logw, k, v, a, g, k_k, k_a, r_k, gn_g, gn_b, chunk, n_sub):
    b, s, width = r.shape
    tm = chunk * n_sub
    seq = pl.BlockSpec((1, tm, width), lambda bi, j: (bi, j, 0))
    par = pl.BlockSpec((1, width), lambda bi, j: (0, 0))
    kern = functools.partial(_rwkv_chunk_kernel, chunk=chunk, n_sub=n_sub)
    return pl.pallas_call(
        kern,
        grid=(b, s // tm),
        in_specs=[seq] * 6 + [par] * 5,
        out_specs=seq,
        out_shape=jax.ShapeDtypeStruct((b, s, width), F32),
        scratch_shapes=[pltpu.VMEM((RWKV_HEADS // 2, LANES, LANES), F32)],
        compiler_params=_params(("arbitrary", "arbitrary")),
        name="rwkv_chunk",
    )(r, logw, k, v, a, g, k_k, k_a, r_k, gn_g, gn_b)


def _layer_norm(x, g, b):
    mu = jnp.mean(x, axis=-1, keepdims=True)
    xc = x - mu
    var = jnp.mean(jnp.square(xc), axis=-1, keepdims=True)
    return xc * lax.rsqrt(var + LN_EPS) * g + b


def _outproj_kernel(x_ref, sb_ref, mla_ref, y_ref, wsb_ref, wmla_ref, wrw_ref, lng_ref, lnb_ref,
                    o_ref, *, alpha):
    mix = _dot(sb_ref[...], wsb_ref[...]) + _dot(mla_ref[...], wmla_ref[...])
    mix = mix + _dot(y_ref[...].astype(BF16), wrw_ref[...])
    o_ref[...] = _layer_norm(alpha * x_ref[...] + mix, lng_ref[...], lnb_ref[...])


def _outproj(x2d, o_sb, o_mla, y, wsb, wmla, wrw, lng, lnb, alpha, tm):
    t, d = x2d.shape
    row = lambda n: pl.BlockSpec((tm, n), lambda i: (i, 0))
    full = lambda w: pl.BlockSpec(w.shape, lambda i: (0, 0))
    return pl.pallas_call(
        functools.partial(_outproj_kernel, alpha=alpha),
        grid=(t // tm,),
        in_specs=[row(d), row(SB_WIDTH), row(MLA_WIDTH), row(RWKV_WIDTH),
                  full(wsb), full(wmla), full(wrw), full(lng), full(lnb)],
        out_specs=row(d),
        out_shape=jax.ShapeDtypeStruct((t, d), F32),
        compiler_params=_params(("parallel",)),
        name="outproj_ln",
    )(x2d, o_sb, o_mla, y, wsb, wmla, wrw, lng, lnb)


def _ffn_kernel(x_ref, wact_ref, wgate_ref, convw_ref, convb_ref, wdown_ref, lng_ref, lnb_ref,
                o_ref, hid_ref, shift_ref, carry_ref, *, alpha, fc):
    tm = x_ref.shape[1]
    d_ff = wact_ref.shape[1]

    @pl.when(pl.program_id(1) == 0)
    def _():
        carry_ref[...] = jnp.zeros(carry_ref.shape, F32)

    x = x_ref[0]
    xb = x.astype(BF16)
    for c in range(d_ff // fc):
        cols = slice(c * fc, (c + 1) * fc)
        u = _dot(xb, wact_ref[:, cols])
        shift_ref[0:SUBLANES, :] = carry_ref[:, cols]
        shift_ref[SUBLANES:, :] = u
        carry_ref[:, cols] = u[tm - SUBLANES:, :]
        prev1 = shift_ref[pl.ds(SUBLANES - 1, tm), :]
        prev2 = shift_ref[pl.ds(SUBLANES - 2, tm), :]
        conv = (prev2 * convw_ref[0:1, cols] + prev1 * convw_ref[1:2, cols] + u * convw_ref[2:3, cols]
                + convb_ref[:, cols])
        gate = _dot(xb, wgate_ref[:, cols])
        gelu = 0.5 * conv * (1.0 + lax.erf(conv * math.sqrt(0.5)))
        hid_ref[:, cols] = (gelu * gate).astype(BF16)
    down = _dot(hid_ref[...], wdown_ref[...])
    o_ref[0] = _layer_norm(alpha * x + down, lng_ref[...], lnb_ref[...])


def _ffn(x, wact, wgate, convw, convb, wdown, lng, lnb, alpha, tm, fc):
    b, s, d = x.shape
    d_ff = wact.shape[1]
    const = lambda w: pl.BlockSpec(w.shape, lambda bi, j: (0, 0), pipeline_mode=pl.Buffered(1))
    tile = pl.BlockSpec((1, tm, d), lambda bi, j: (bi, j, 0))
    return pl.pallas_call(
        functools.partial(_ffn_kernel, alpha=alpha, fc=fc),
        grid=(b, s // tm),
        in_specs=[tile, const(wact), const(wgate), const(convw), const(convb), const(wdown),
                  const(lng), const(lnb)],
        out_specs=tile,
        out_shape=jax.ShapeDtypeStruct((b, s, d), F32),
        scratch_shapes=[pltpu.VMEM((tm, d_ff), BF16),
                        pltpu.VMEM((tm + SUBLANES, fc), F32),
                        pltpu.VMEM((SUBLANES, d_ff), F32)],
        compiler_params=_params(("arbitrary", "arbitrary")),
        name="conv_ffn",
    )(x, wact, wgate, convw, convb, wdown, lng, lnb)


def _rope_tables(s):
    half = MLA_ROPE_DIM // 2
    inv_freq = 1.0 / (ROPE_THETA ** (jnp.arange(0, MLA_ROPE_DIM, 2, dtype=F32) / MLA_ROPE_DIM))
    ang = jnp.arange(s, dtype=F32)[:, None] * inv_freq[None, :]
    cos, sin = jnp.cos(ang), jnp.sin(ang)
    cos2 = jnp.concatenate([cos, cos], axis=-1)
    sin2 = jnp.concatenate([-sin, sin], axis=-1)
    pad = jnp.zeros((s, MLA_PAD_DIM - MLA_NOPE_DIM - MLA_ROPE_DIM), F32)
    qcos = jnp.tile(jnp.concatenate([jnp.ones((s, MLA_NOPE_DIM), F32), cos2, pad], axis=-1), (1, MLA_HEADS))
    qsin = jnp.tile(jnp.concatenate([jnp.zeros((s, MLA_NOPE_DIM), F32), sin2, pad], axis=-1), (1, MLA_HEADS))
    ktab = jnp.concatenate([cos2, sin2], axis=-1)
    del half
    return qcos, qsin, ktab


def _swap_halves(w):
    half = w.shape[-1] // 2
    return jnp.concatenate([w[..., half:], w[..., :half]], axis=-1)


def _mla_weights(w_uq, w_ukv):
    qd = MLA_NOPE_DIM + MLA_ROPE_DIM
    pad = MLA_PAD_DIM - qd
    uq = w_uq.reshape(MLA_Q_LORA, MLA_HEADS, qd)
    zq = jnp.zeros((MLA_Q_LORA, MLA_HEADS, pad), F32)
    wq = jnp.concatenate([uq, zq], axis=-1)
    wqr = jnp.concatenate([jnp.zeros((MLA_Q_LORA, MLA_HEADS, MLA_NOPE_DIM), F32),
                           _swap_halves(uq[..., MLA_NOPE_DIM:]), zq], axis=-1)
    ukv = w_ukv.reshape(MLA_KV_LORA, MLA_HEADS, MLA_NOPE_DIM + HEAD_DIM)
    wkn = jnp.concatenate([ukv[..., :MLA_NOPE_DIM],
                           jnp.zeros((MLA_KV_LORA, MLA_HEADS, MLA_PAD_DIM - MLA_NOPE_DIM), F32)], axis=-1)
    wv = ukv[..., MLA_NOPE_DIM:]
    eye = jnp.eye(MLA_ROPE_DIM, dtype=F32)[:, None, :]
    scat = jnp.concatenate([jnp.zeros((MLA_ROPE_DIM, MLA_HEADS, MLA_NOPE_DIM), F32),
                            jnp.broadcast_to(eye, (MLA_ROPE_DIM, MLA_HEADS, MLA_ROPE_DIM)),
                            jnp.zeros((MLA_ROPE_DIM, MLA_HEADS, pad), F32)], axis=-1)
    scat = jnp.concatenate([scat, scat], axis=0)
    flat = lambda w: w.reshape(w.shape[0], -1).astype(BF16)
    return flat(wq), flat(wqr), flat(wkn), flat(wv), flat(scat)


def _layer(x, w_in, mla_q_norm, mla_w_uq, mla_kv_norm, mla_w_ukv, rwkv_mu, rwkv_w0, rwkv_w2, rwkv_a0, rwkv_a2,
           rwkv_g2, rwkv_k_k, rwkv_k_a, rwkv_r_k, rwkv_gn_g, rwkv_gn_b, w_o, ln1_g, ln1_b,
           ffn_w_up, ffn_conv_w, ffn_conv_b, ffn_w_down, ln2_g, ln2_b, *, alpha, rope):
    b, s, d = x.shape
    t = b * s
    x2d = x.reshape(t, d)
    row_tile = min(512, s)
    attn_blk = min(256, s)

    c0 = SB_COLS
    c1 = c0 + MLA_Q_LORA
    c2 = c1 + MLA_KV_LORA
    c3 = c2 + MLA_ROPE_DIM
    w_kr = w_in[:, c2:c3]
    wkr2 = jnp.concatenate([w_kr, _swap_halves(w_kr)], axis=-1)
    bf = lambda w: w.astype(BF16)
    q_sb, k_sb, v_sb, cq, ckv, kr, hr = _inproj(
        x2d, bf(w_in[:, :c0]), bf(w_in[:, c0:c1]), bf(w_in[:, c1:c2]), bf(wkr2), bf(w_in[:, c3:]), row_tile)

    o_sb = _sb_attention(q_sb.reshape(b, s, -1), k_sb.reshape(b, s, -1), v_sb.reshape(b, s, -1),
                         attn_blk, attn_blk)

    wq, wqr, wkn, wv, scat = _mla_weights(mla_w_uq, mla_w_ukv)
    qcos, qsin, ktab = rope
    q_m, k_m, v_m = _mla_prep(cq, ckv, kr, mla_q_norm[None, :], mla_kv_norm[None, :], wq, wqr, wkn, wv, scat,
                              qcos, qsin, ktab, s, row_tile)
    o_mla = _mla_attention(q_m.reshape(b, s, -1), k_m.reshape(b, s, -1), v_m.reshape(b, s, -1), attn_blk)

    zl = jnp.zeros((RWKV_DECAY_LORA, RWKV_WIDTH), F32)
    wlora = jnp.concatenate([jnp.concatenate([rwkv_w2, zl], axis=1),
                             jnp.concatenate([zl, rwkv_a2], axis=1)], axis=0)
    r, w, k, v, a, g = _rwkv_prep(hr.reshape(b, s, -1), rwkv_mu[None, :], rwkv_w0[None, :], rwkv_a0[None, :],
                                  bf(wlora), bf(rwkv_g2), row_tile)
    y = _rwkv_chunked(r, w, k, v, a, g, rwkv_k_k[None, :], rwkv_k_a[None, :], rwkv_r_k.reshape(1, -1),
                      rwkv_gn_g[None, :], rwkv_gn_b[None, :], RWKV_CHUNK, min(4, s // RWKV_CHUNK))

    x1 = _outproj(x2d, o_sb.reshape(t, -1), o_mla.reshape(t, -1), y.reshape(t, -1),
                  bf(w_o[:SB_WIDTH]), bf(w_o[SB_WIDTH:SB_WIDTH + MLA_WIDTH]), bf(w_o[SB_WIDTH + MLA_WIDTH:]),
                  ln1_g[None, :], ln1_b[None, :], alpha, row_tile)

    d_ff = ffn_w_down.shape[0]
    return _ffn(x1.reshape(b, s, d), bf(ffn_w_up[:, :d_ff]), bf(ffn_w_up[:, d_ff:]), ffn_conv_w,
                ffn_conv_b[None, :], bf(ffn_w_down), ln2_g[None, :], ln2_b[None, :], alpha, min(256, s), 256)


def kernel(x, w_in, mla_q_norm, mla_w_uq, mla_kv_norm, mla_w_ukv, rwkv_mu, rwkv_w0, rwkv_w2, rwkv_a0, rwkv_a2, rwkv_g2, rwkv_k_k, rwkv_k_a, rwkv_r_k, rwkv_gn_g, rwkv_gn_b, w_o, ln1_g, ln1_b, ffn_w_up, ffn_conv_w, ffn_conv_b, ffn_w_down, ln2_g, ln2_b):
    depth = w_in.shape[0]
    alpha = (2 * depth) ** 0.25
    rope = _rope_tables(x.shape[1])
    weights = (w_in, mla_q_norm, mla_w_uq, mla_kv_norm, mla_w_ukv, rwkv_mu, rwkv_w0, rwkv_w2, rwkv_a0, rwkv_a2,
               rwkv_g2, rwkv_k_k, rwkv_k_a, rwkv_r_k, rwkv_gn_g, rwkv_gn_b, w_o, ln1_g, ln1_b,
               ffn_w_up, ffn_conv_w, ffn_conv_b, ffn_w_down, ln2_g, ln2_b)
    for layer in range(depth):
        x = _layer(x, *(w[layer] for w in weights), alpha=alpha, rope=rope)
    return x
```

```python
import functools
import math

import jax
import jax.numpy as jnp
from jax import lax
from jax.experimental import pallas as pl
from jax.experimental.pallas import tpu as pltpu

F32 = jnp.float32
BF16 = jnp.bfloat16

HEAD_DIM = 64
SB_HEADS = 4
MLA_HEADS = 4
RWKV_HEADS = 8
SB_WIDTH = SB_HEADS * HEAD_DIM
MLA_WIDTH = MLA_HEADS * HEAD_DIM
RWKV_WIDTH = RWKV_HEADS * HEAD_DIM
MLA_Q_LORA = 192
MLA_KV_LORA = 128
MLA_NOPE_DIM = 64
MLA_ROPE_DIM = 32
MLA_PAD_DIM = 128
ROPE_THETA = 10000.0
RWKV_DECAY_LORA = 64
RWKV_AAA_LORA = 64
RWKV_GATE_LORA = 128
RWKV_GN_EPS = 64e-5
RWKV_CHUNK = 64
SB_COLS = 3 * SB_WIDTH
MLA_COLS = MLA_Q_LORA + MLA_KV_LORA + MLA_ROPE_DIM
RWKV_COLS = 3 * RWKV_WIDTH + RWKV_DECAY_LORA + RWKV_AAA_LORA + RWKV_GATE_LORA
CONV_WIDTH = 3
LN_EPS = 1e-5
RMS_EPS = 1e-6

LANES = 128
SUBLANES = 8
VMEM_LIMIT = 48 * 1024 * 1024


def _dot(a, b):
    return jnp.dot(a, b, preferred_element_type=F32)


def _dot_nt(a, b):
    return lax.dot_general(a, b, (((1,), (1,)), ((), ())), preferred_element_type=F32)


def _dot_tn(a, b):
    return lax.dot_general(a, b, (((0,), (0,)), ((), ())), preferred_element_type=F32)


def _split_bf16(x, terms):
    out = []
    rem = x
    for _ in range(terms):
        part = rem.astype(BF16)
        out.append(part)
        rem = rem - part.astype(F32)
    return out


def _params(sem):
    return pltpu.CompilerParams(dimension_semantics=sem, vmem_limit_bytes=VMEM_LIMIT)


def _inproj_kernel(x_ref, wsb_ref, wcq_ref, wckv_ref, wkr_ref, wrw_ref,
                   q_ref, k_ref, v_ref, cq_ref, ckv_ref, kr_ref, hr_ref):
    xb = x_ref[...].astype(BF16)
    sb = _dot(xb, wsb_ref[...])
    q_ref[...] = sb[:, :SB_WIDTH].astype(BF16)
    k_ref[...] = sb[:, SB_WIDTH:2 * SB_WIDTH].astype(BF16)
    v_ref[...] = sb[:, 2 * SB_WIDTH:].astype(BF16)
    cq_ref[...] = _dot(xb, wcq_ref[...])
    ckv_ref[...] = _dot(xb, wckv_ref[...])
    kr_ref[...] = _dot(xb, wkr_ref[...])
    hr_ref[...] = _dot(xb, wrw_ref[...])


def _inproj(x2d, wsb, wcq, wckv, wkr, wrw, tm):
    t, d = x2d.shape
    row = lambda n: pl.BlockSpec((tm, n), lambda i: (i, 0))
    full = lambda w: pl.BlockSpec(w.shape, lambda i: (0, 0))
    widths = (SB_WIDTH, SB_WIDTH, SB_WIDTH, MLA_Q_LORA, MLA_KV_LORA, 2 * MLA_ROPE_DIM, RWKV_COLS)
    dtypes = (BF16, BF16, BF16, F32, F32, F32, F32)
    return pl.pallas_call(
        _inproj_kernel,
        grid=(t // tm,),
        in_specs=[row(d), full(wsb), full(wcq), full(wckv), full(wkr), full(wrw)],
        out_specs=[row(n) for n in widths],
        out_shape=[jax.ShapeDtypeStruct((t, n), dt) for n, dt in zip(widths, dtypes)],
        compiler_params=_params(("parallel",)),
        name="inproj",
    )(x2d, wsb, wcq, wckv, wkr, wrw)


def _sb_attn_kernel(q_ref, k_ref, v_ref, o_ref, *, bq, bk, scale):
    i = pl.program_id(2)
    ratio = bk // bq
    q = q_ref[0]
    lane = lax.broadcasted_iota(jnp.int32, (bq, LANES), 1)
    urow = lax.broadcasted_iota(jnp.int32, (bk, bk), 0)
    ucol = lax.broadcasted_iota(jnp.int32, (bk, bk), 1)
    upper = jnp.where(urow > ucol, 1.0, 0.0).astype(BF16)
    row = lax.broadcasted_iota(jnp.int32, (bq, bk), 0) + (i % ratio) * bq
    col = lax.broadcasted_iota(jnp.int32, (bq, bk), 1)
    past_diag = col < row
    head_masks = [lane < HEAD_DIM, lane >= HEAD_DIM]
    assert math.frexp(scale)[0] == 0.5
    neg_q = q * jnp.asarray(-scale, q.dtype)
    qh = [jnp.where(m, neg_q, jnp.zeros_like(q)) for m in head_masks]

    def blocks(js, diags, carry):
        chains = [(n, h) for n in range(len(js)) for h in range(2)]
        starts = [pl.multiple_of(j * bk, bk) for j in js]
        kb = [k_ref[0, pl.ds(st, bk), :] for st in starts]
        vb = [v_ref[0, pl.ds(st, bk), :] for st in starts]
        u = {ch: _dot_nt(qh[ch[1]], kb[ch[0]]) for ch in chains}
        log_keep, split = {}, {}
        for ch in chains:
            soft = jnp.log(1.0 + jnp.exp(-jnp.abs(u[ch])))
            lk = jnp.minimum(u[ch], 0.0) - soft
            if diags[ch[0]]:
                lk = jnp.where(past_diag, lk, 0.0)
            log_keep[ch] = lk
            split[ch] = _split_bf16(lk, 2)
        later = {ch: _dot(split[ch][0], upper) + _dot(split[ch][1], upper) for ch in chains}
        run = [carry[1], carry[3]]
        w = {}
        for ch in chains:
            n, h = ch
            logw = (log_keep[ch] - u[ch]) + later[ch] + run[h]
            wch = jnp.exp(logw)
            if diags[n]:
                wch = jnp.where(past_diag, wch, 0.0)
            w[ch] = wch.astype(BF16)
            run[h] = run[h] + jnp.sum(log_keep[ch], axis=1, keepdims=True)
        acc = [carry[0], carry[2]]
        for ch in chains:
            acc[ch[1]] = acc[ch[1]] + _dot(w[ch], vb[ch[0]])
        return (acc[0], run[0], acc[1], run[1])

    zero_acc = jnp.zeros((bq, LANES), F32)
    zero_run = jnp.zeros((bq, 1), F32)
    init = (zero_acc, zero_run, zero_acc, zero_run)
    jd = i // ratio
    carry = lax.cond(jd % 2 == 1,
                     lambda c: blocks([jd, jd - 1], [True, False], c),
                     lambda c: blocks([jd], [True], c), init)
    rest = jd - jd % 2

    def two_blocks(s, c):
        j = rest - 1 - 2 * s
        return blocks([j, j - 1], [False, False], c)

    carry = lax.fori_loop(0, rest // 2, two_blocks, carry)
    o_ref[0] = jnp.where(head_masks[0], carry[0], carry[2]).astype(o_ref.dtype)


def _sb_attention(q, k, v, bq, bk):
    b, s, w = q.shape
    pairs = w // LANES
    kern = functools.partial(_sb_attn_kernel, bq=bq, bk=bk, scale=HEAD_DIM ** -0.5)
    return pl.pallas_call(
        kern,
        grid=(b, pairs, s // bq),
        in_specs=[pl.BlockSpec((1, bq, LANES), lambda bi, p, i: (bi, i, p)),
                  pl.BlockSpec((1, s, LANES), lambda bi, p, i: (bi, 0, p)),
                  pl.BlockSpec((1, s, LANES), lambda bi, p, i: (bi, 0, p))],
        out_specs=pl.BlockSpec((1, bq, LANES), lambda bi, p, i: (bi, i, p)),
        out_shape=jax.ShapeDtypeStruct((b, s, w), BF16),
        compiler_params=_params(("parallel", "parallel", "arbitrary")),
        name="sb_attn",
    )(q, k, v)


def _mla_prep_kernel(cq_ref, ckv_ref, kr_ref, qn_ref, kvn_ref, wq_ref, wqr_ref, wkn_ref, wv_ref, scat_ref,
                     qcos_ref, qsin_ref, ktab_ref, q_out, k_out, v_out):
    cq = cq_ref[...]
    cq = cq * lax.rsqrt(jnp.mean(jnp.square(cq), axis=-1, keepdims=True) + RMS_EPS) * qn_ref[...]
    cqb = cq.astype(BF16)
    q = _dot(cqb, wq_ref[...]) * qcos_ref[...] + _dot(cqb, wqr_ref[...]) * qsin_ref[...]
    q_out[...] = q.astype(BF16)
    ckv = ckv_ref[...]
    ckv = ckv * lax.rsqrt(jnp.mean(jnp.square(ckv), axis=-1, keepdims=True) + RMS_EPS) * kvn_ref[...]
    ckvb = ckv.astype(BF16)
    v_out[...] = _dot(ckvb, wv_ref[...]).astype(BF16)
    prod = kr_ref[...] * ktab_ref[...]
    hi, lo = _split_bf16(prod, 2)
    rope = _dot(hi, scat_ref[...]) + _dot(lo, scat_ref[...])
    k_out[...] = (_dot(ckvb, wkn_ref[...]) + rope).astype(BF16)


def _mla_prep(cq, ckv, kr, qn, kvn, wq, wqr, wkn, wv, scat, qcos, qsin, ktab, s, tm):
    t = cq.shape[0]
    nj = s // tm
    row = lambda n: pl.BlockSpec((tm, n), lambda i: (i, 0))
    full = lambda w: pl.BlockSpec(w.shape, lambda i: (0, 0))
    pos = lambda n: pl.BlockSpec((tm, n), lambda i: (i % nj, 0))
    qk_w = MLA_HEADS * MLA_PAD_DIM
    return pl.pallas_call(
        _mla_prep_kernel,
        grid=(t // tm,),
        in_specs=[row(MLA_Q_LORA), row(MLA_KV_LORA), row(2 * MLA_ROPE_DIM), full(qn), full(kvn), full(wq),
                  full(wqr), full(wkn), full(wv), full(scat), pos(qk_w), pos(qk_w), pos(2 * MLA_ROPE_DIM)],
        out_specs=[row(qk_w), row(qk_w), row(MLA_WIDTH)],
        out_shape=[jax.ShapeDtypeStruct((t, qk_w), BF16), jax.ShapeDtypeStruct((t, qk_w), BF16),
                   jax.ShapeDtypeStruct((t, MLA_WIDTH), BF16)],
        compiler_params=_params(("parallel",)),
        name="mla_prep",
    )(cq, ckv, kr, qn, kvn, wq, wqr, wkn, wv, scat, qcos, qsin, ktab)


def _mla_attn_kernel(q_ref, k_ref, v_ref, o_ref, *, blk, scale):
    i = pl.program_id(2)
    q = q_ref[0]
    qh = [q[:, :MLA_PAD_DIM], q[:, MLA_PAD_DIM:]]
    row = lax.broadcasted_iota(jnp.int32, (blk, blk), 0)
    col = lax.broadcasted_iota(jnp.int32, (blk, blk), 1)
    causal_diag = col <= row
    lane = lax.broadcasted_iota(jnp.int32, (blk, LANES), 1)

    def blocks(js, diags, carry):
        starts = [pl.multiple_of(j * blk, blk) for j in js]
        kb = [k_ref[0, pl.ds(st, blk), :] for st in starts]
        vb = [v_ref[0, pl.ds(st, blk), :] for st in starts]
        nb = range(len(js))
        sc = [[_dot_nt(qh[h], kb[n][:, h * MLA_PAD_DIM:(h + 1) * MLA_PAD_DIM]) for n in nb] for h in range(2)]
        new = []
        for h in range(2):
            m, l, acc = carry[3 * h:3 * h + 3]
            s_h = [jnp.where(causal_diag, sc[h][n], -jnp.inf) if diags[n] else sc[h][n] for n in nb]
            m_new = m
            for n in nb:
                m_new = jnp.maximum(m_new, jnp.max(s_h[n], axis=1, keepdims=True))
            alpha = jnp.exp2((m - m_new) * exp2_scale)
            p = [jnp.exp2((s_h[n] - m_new) * exp2_scale) for n in nb]
            l = alpha * l
            acc = alpha * acc
            for n in nb:
                l = l + jnp.sum(p[n], axis=1, keepdims=True)
            for n in nb:
                acc = acc + _dot(p[n].astype(BF16), vb[n])
            new += [m_new, l, acc]
        return tuple(new)

    exp2_scale = scale * math.log2(math.e)
    init = (jnp.full((blk, 1), -jnp.inf, F32), jnp.zeros((blk, 1), F32), jnp.zeros((blk, LANES), F32)) * 2
    carry = lax.fori_loop(0, i // 2, lambda s, c: blocks([2 * s, 2 * s + 1], [False, False], c), init)
    carry = lax.cond(i % 2 == 1,
                     lambda c: blocks([i - 1, i], [False, True], c),
                     lambda c: blocks([i], [True], c), carry)
    out = jnp.where(lane < HEAD_DIM, carry[2] / carry[1], carry[5] / carry[4])
    o_ref[0] = out.astype(o_ref.dtype)


def _mla_attention(q, k, v, blk):
    b, s, _ = q.shape
    pairs = MLA_HEADS // 2
    kern = functools.partial(_mla_attn_kernel, blk=blk, scale=(MLA_NOPE_DIM + MLA_ROPE_DIM) ** -0.5)
    return pl.pallas_call(
        kern,
        grid=(b, pairs, s // blk),
        in_specs=[pl.BlockSpec((1, blk, 2 * MLA_PAD_DIM), lambda bi, p, i: (bi, i, p)),
                  pl.BlockSpec((1, s, 2 * MLA_PAD_DIM), lambda bi, p, i: (bi, 0, p)),
                  pl.BlockSpec((1, s, LANES), lambda bi, p, i: (bi, 0, p))],
        out_specs=pl.BlockSpec((1, blk, LANES), lambda bi, p, i: (bi, i, p)),
        out_shape=jax.ShapeDtypeStruct((b, s, MLA_WIDTH), BF16),
        compiler_params=_params(("parallel", "parallel", "arbitrary")),
        name="mla_attn",
    )(q, k, v)


def _rwkv_prep_kernel(h_ref, mu_ref, w0_ref, a0_ref, wlora_ref, g2_ref,
                      r_out, w_out, k_out, v_out, a_out, g_out, shift_ref):
    tm = h_ref.shape[1]
    c = RWKV_WIDTH

    @pl.when(pl.program_id(1) == 0)
    def _():
        shift_ref[0:SUBLANES, :] = jnp.zeros((SUBLANES, RWKV_COLS), F32)

    h = h_ref[0]
    shift_ref[SUBLANES:, :] = h
    prev = shift_ref[pl.ds(SUBLANES - 1, tm), :]
    shift_ref[0:SUBLANES, :] = h[tm - SUBLANES:, :]
    p = h + (prev - h) * mu_ref[...]
    r_out[0] = p[:, :c]
    k_out[0] = p[:, c:2 * c]
    v_out[0] = p[:, 2 * c:3 * c]
    lora_in = p[:, 3 * c:3 * c + LANES]
    lane = lax.broadcasted_iota(jnp.int32, lora_in.shape, 1)
    lora_in = jnp.where(lane < RWKV_DECAY_LORA, jnp.tanh(lora_in), lora_in)
    lora = _dot(lora_in.astype(BF16), wlora_ref[...])
    w = -jax.nn.softplus(-(w0_ref[...] + lora[:, :c])) - 0.5
    w_out[0] = -jnp.exp(w)
    a_out[0] = jax.nn.sigmoid(a0_ref[...] + lora[:, c:])
    gate_in = jax.nn.sigmoid(p[:, 3 * c + LANES:])
    g_out[0] = _dot(gate_in.astype(BF16), g2_ref[...])


def _rwkv_prep(hr, mu, w0, a0, wlora, g2, tm):
    b, s, _ = hr.shape
    full = lambda w: pl.BlockSpec(w.shape, lambda bi, j: (0, 0))
    out = pl.BlockSpec((1, tm, RWKV_WIDTH), lambda bi, j: (bi, j, 0))
    return pl.pallas_call(
        _rwkv_prep_kernel,
        grid=(b, s // tm),
        in_specs=[pl.BlockSpec((1, tm, RWKV_COLS), lambda bi, j: (bi, j, 0)),
                  full(mu), full(w0), full(a0), full(wlora), full(g2)],
        out_specs=[out] * 6,
        out_shape=[jax.ShapeDtypeStruct((b, s, RWKV_WIDTH), F32)] * 6,
        scratch_shapes=[pltpu.VMEM((tm + SUBLANES, RWKV_COLS), F32)],
        compiler_params=_params(("arbitrary", "arbitrary")),
        name="rwkv_prep",
    )(hr, mu, w0, a0, wlora, g2)


def _mm(a, b, dot=_dot):
    return dot(a.astype(BF16), b.astype(BF16))


def _rwkv_chunk_kernel(r_ref, lw_ref, k_ref, v_ref, a_ref, g_ref, kk_ref, ka_ref, rk_ref, gng_ref, gnb_ref,
                       y_ref, state_ref, *, chunk, n_sub):
    c = chunk
    n = HEAD_DIM
    pairs = RWKV_HEADS // 2

    @pl.when(pl.program_id(1) == 0)
    def _():
        state_ref[...] = jnp.zeros(state_ref.shape, F32)

    rowi = lax.broadcasted_iota(jnp.int32, (c, c), 0)
    coli = lax.broadcasted_iota(jnp.int32, (c, c), 1)
    tri_incl = jnp.where(rowi >= coli, 1.0, 0.0).astype(BF16)
    r2 = lax.broadcasted_iota(jnp.int32, (2 * c, 2 * c), 0)
    c2 = lax.broadcasted_iota(jnp.int32, (2 * c, 2 * c), 1)
    same_head = (r2 >= c) == (c2 >= c)
    strict = jnp.logical_and(same_head, r2 > c2)
    incl = jnp.logical_and(same_head, r2 >= c2)
    eye2 = jnp.where(r2 == c2, 1.0, 0.0).astype(F32)
    lane = lax.broadcasted_iota(jnp.int32, (c, LANES), 1)
    m0 = lane < n
    own = jnp.concatenate([m0, jnp.logical_not(m0)], axis=0)
    lr = lax.broadcasted_iota(jnp.int32, (LANES, LANES), 0)
    lc = lax.broadcasted_iota(jnp.int32, (LANES, LANES), 1)
    blockdiag = (lr >= n) == (lc >= n)

    def stack(x):
        return jnp.where(own, jnp.concatenate([x, x], axis=0), 0.0)

    def unstack(x):
        return jnp.where(m0, x[:c], x[c:])

    for sub in range(n_sub):
        rows = pl.ds(sub * c, c)
        r = r_ref[0, rows, :]
        logw = lw_ref[0, rows, :]
        k = k_ref[0, rows, :]
        v = v_ref[0, rows, :]
        a = a_ref[0, rows, :]
        kk = k * kk_ref[...]
        kp = k * (1.0 + (a - 1.0) * ka_ref[...])
        bon = r * kp * rk_ref[...]
        l3 = _split_bf16(logw, 3)
        cum = _dot(tri_incl, l3[0]) + _dot(tri_incl, l3[1]) + _dot(tri_incl, l3[2])
        g_in = jnp.exp(cum)
        g_ex = jnp.exp(cum - logw)
        g_inv = jnp.exp(-cum)
        rt = r * g_in
        kt = kp * g_inv
        ag = a * g_inv
        g_end = g_in[c - 1:c, :]
        P = range(pairs)
        ln = [slice(p * LANES, (p + 1) * LANES) for p in P]
        KK = [stack(kk[:, ln[p]]) for p in P]
        KK = [x * lax.rsqrt(jnp.maximum(jnp.sum(jnp.square(x), axis=1, keepdims=True), 1e-12)) for x in KK]
        A = [-KK[p] * stack(g_ex[:, ln[p]]) for p in P]
        B = [KK[p] * stack(ag[:, ln[p]]) for p in P]
        R = [stack(rt[:, ln[p]]) for p in P]
        K = [stack(kt[:, ln[p]]) for p in P]
        V = [stack(v[:, ln[p]]) for p in P]
        bonus = [jnp.sum(stack(bon[:, ln[p]]), axis=1, keepdims=True) * V[p] for p in P]
        big = [_mm(jnp.concatenate([A[p], R[p]], axis=0), jnp.concatenate([B[p], K[p]], axis=0), _dot_nt)
               for p in P]
        a_ab = [jnp.where(strict, big[p][:2 * c, :2 * c], 0.0) for p in P]
        a_ak = [jnp.where(strict, big[p][:2 * c, 2 * c:], 0.0) for p in P]
        a_rb = [jnp.where(incl, big[p][2 * c:, :2 * c], 0.0) for p in P]
        a_rk = [jnp.where(incl, big[p][2 * c:, 2 * c:], 0.0) for p in P]
        T = [eye2 + a_ab[p] for p in P]
        Lp = [_mm(a_ab[p], a_ab[p]) for p in P]
        for _ in range(int(math.log2(c)) - 2):
            X = [_mm(jnp.concatenate([T[p], Lp[p]], axis=0), Lp[p]) for p in P]
            T = [T[p] + X[p][:2 * c] for p in P]
            Lp = [X[p][2 * c:] for p in P]
        T = [T[p] + _mm(T[p], Lp[p]) for p in P]
        akv = [_mm(a_ak[p], V[p]) for p in P]
        W = [_mm(T[p], jnp.concatenate([A[p], akv[p]], axis=1)) for p in P]
        ys = []
        for p in P:
            H0T = state_ref[p]
            U = _mm(W[p][:, :LANES], H0T, _dot_nt) + W[p][:, LANES:]
            Y = (_mm(R[p], H0T, _dot_nt)
                 + _mm(jnp.concatenate([a_rb[p], a_rk[p]], axis=1), jnp.concatenate([U, V[p]], axis=0)))
            upd = _mm(jnp.concatenate([U, V[p]], axis=0), jnp.concatenate([B[p], K[p]], axis=0), _dot_tn)
            state_ref[p] = (H0T + jnp.where(blockdiag, upd, 0.0)) * g_end[:, ln[p]]
            mean = jnp.sum(Y, axis=1, keepdims=True) * (1.0 / n)
            yc = jnp.where(own, Y - mean, 0.0)
            var = jnp.sum(jnp.square(yc), axis=1, keepdims=True) * (1.0 / n)
            ys.append((unstack(yc * lax.rsqrt(var + RWKV_GN_EPS)), unstack(bonus[p])))
        yn = jnp.concatenate([t[0] for t in ys], axis=1)
        bo = jnp.concatenate([t[1] for t in ys], axis=1)
        y_ref[0, rows, :] = (yn * gng_ref[...] + gnb_ref[...] + bo) * g_ref[0, rows, :]


def _rwkv_chunked(r, logw, k, v, a, g, k_k, k_a, r_k, gn_g, gn_b, chunk, n_sub):
    b, s, width = r.shape
    tm = chunk * n_sub
    seq = pl.BlockSpec((1, tm, width), lambda bi, j: (bi, j, 0))
    par = pl.BlockSpec((1, width), lambda bi, j: (0, 0))
    kern = functools.partial(_rwkv_chunk_kernel, chunk=chunk, n_sub=n_sub)
    return pl.pallas_call(
        kern,
        grid=(b, s // tm),
        in_specs=[seq] * 6 + [par] * 5,
        out_specs=seq,
        out_shape=jax.ShapeDtypeStruct((b, s, width), F32),
        scratch_shapes=[pltpu.VMEM((RWKV_HEADS // 2, LANES, LANES), F32)],
        compiler_params=_params(("arbitrary", "arbitrary")),
        name="rwkv_chunk",
    )(r, logw, k, v, a, g, k_k, k_a, r_k, gn_g, gn_b)


def _layer_norm(x, g, b):
    mu = jnp.mean(x, axis=-1, keepdims=True)
    xc = x - mu
    var = jnp.mean(jnp.square(xc), axis=-1, keepdims=True)
    return xc * lax.rsqrt(var + LN_EPS) * g + b


def _outproj_kernel(x_ref, sb_ref, mla_ref, y_ref, wsb_ref, wmla_ref, wrw_ref, lng_ref, lnb_ref,
                    o_ref, *, alpha):
    mix = _dot(sb_ref[...], wsb_ref[...]) + _dot(mla_ref[...], wmla_ref[...])
    mix = mix + _dot(y_ref[...].astype(BF16), wrw_ref[...])
    o_ref[...] = _layer_norm(alpha * x_ref[...] + mix, lng_ref[...], lnb_ref[...])


def _outproj(x2d, o_sb, o_mla, y, wsb, wmla, wrw, lng, lnb, alpha, tm):
    t, d = x2d.shape
    row = lambda n: pl.BlockSpec((tm, n), lambda i: (i, 0))
    full = lambda w: pl.BlockSpec(w.shape, lambda i: (0, 0))
    return pl.pallas_call(
        functools.partial(_outproj_kernel, alpha=alpha),
        grid=(t // tm,),
        in_specs=[row(d), row(SB_WIDTH), row(MLA_WIDTH), row(RWKV_WIDTH),
                  full(wsb), full(wmla), full(wrw), full(lng), full(lnb)],
        out_specs=row(d),
        out_shape=jax.ShapeDtypeStruct((t, d), F32),
        compiler_params=_params(("parallel",)),
        name="outproj_ln",
    )(x2d, o_sb, o_mla, y, wsb, wmla, wrw, lng, lnb)


def _ffn_kernel(x_ref, wact_ref, wgate_ref, convw_ref, convb_ref, wdown_ref, lng_ref, lnb_ref,
                o_ref, hid_ref, shift_ref, carry_ref, *, alpha, fc):
    tm = x_ref.shape[1]
    d_ff = wact_ref.shape[1]

    @pl.when(pl.program_id(1) == 0)
    def _():
        carry_ref[...] = jnp.zeros(carry_ref.shape, F32)

    x = x_ref[0]
    xb = x.astype(BF16)
    for c in range(d_ff // fc):
        cols = slice(c * fc, (c + 1) * fc)
        u = _dot(xb, wact_ref[:, cols])
        shift_ref[0:SUBLANES, :] = carry_ref[:, cols]
        shift_ref[SUBLANES:, :] = u
        carry_ref[:, cols] = u[tm - SUBLANES:, :]
        prev1 = shift_ref[pl.ds(SUBLANES - 1, tm), :]
        prev2 = shift_ref[pl.ds(SUBLANES - 2, tm), :]
        conv = (prev2 * convw_ref[0:1, cols] + prev1 * convw_ref[1:2, cols] + u * convw_ref[2:3, cols]
                + convb_ref[:, cols])
        gate = _dot(xb, wgate_ref[:, cols])
        gelu = 0.5 * conv * (1.0 + lax.erf(conv * math.sqrt(0.5)))
        hid_ref[:, cols] = (gelu * gate).astype(BF16)
    down = _dot(hid_ref[...], wdown_ref[...])
    o_ref[0] = _layer_norm(alpha * x + down, lng_ref[...], lnb_ref[...])


def _ffn(x, wact, wgate, convw, convb, wdown, lng, lnb, alpha, tm, fc):
    b, s, d = x.shape
    d_ff = wact.shape[1]
    const = lambda w: pl.BlockSpec(w.shape, lambda bi, j: (0, 0), pipeline_mode=pl.Buffered(1))
    tile = pl.BlockSpec((1, tm, d), lambda bi, j: (bi, j, 0))
    return pl.pallas_call(
        functools.partial(_ffn_kernel, alpha=alpha, fc=fc),
        grid=(b, s // tm),
        in_specs=[tile, const(wact), const(wgate), const(convw), const(convb), const(wdown),
                  const(lng), const(lnb)],
        out_specs=tile,
        out_shape=jax.ShapeDtypeStruct((b, s, d), F32),
        scratch_shapes=[pltpu.VMEM((tm, d_ff), BF16),
                        pltpu.VMEM((tm + SUBLANES, fc), F32),
                        pltpu.VMEM((SUBLANES, d_ff), F32)],
        compiler_params=_params(("arbitrary", "arbitrary")),
        name="conv_ffn",
    )(x, wact, wgate, convw, convb, wdown, lng, lnb)


def _rope_tables(s):
    half = MLA_ROPE_DIM // 2
    inv_freq = 1.0 / (ROPE_THETA ** (jnp.arange(0, MLA_ROPE_DIM, 2, dtype=F32) / MLA_ROPE_DIM))
    ang = jnp.arange(s, dtype=F32)[:, None] * inv_freq[None, :]
    cos, sin = jnp.cos(ang), jnp.sin(ang)
    cos2 = jnp.concatenate([cos, cos], axis=-1)
    sin2 = jnp.concatenate([-sin, sin], axis=-1)
    pad = jnp.zeros((s, MLA_PAD_DIM - MLA_NOPE_DIM - MLA_ROPE_DIM), F32)
    qcos = jnp.tile(jnp.concatenate([jnp.ones((s, MLA_NOPE_DIM), F32), cos2, pad], axis=-1), (1, MLA_HEADS))
    qsin = jnp.tile(jnp.concatenate([jnp.zeros((s, MLA_NOPE_DIM), F32), sin2, pad], axis=-1), (1, MLA_HEADS))
    ktab = jnp.concatenate([cos2, sin2], axis=-1)
    del half
    return qcos, qsin, ktab


def _swap_halves(w):
    half = w.shape[-1] // 2
    return jnp.concatenate([w[..., half:], w[..., :half]], axis=-1)


def _mla_weights(w_uq, w_ukv):
    qd = MLA_NOPE_DIM + MLA_ROPE_DIM
    pad = MLA_PAD_DIM - qd
    uq = w_uq.reshape(MLA_Q_LORA, MLA_HEADS, qd)
    zq = jnp.zeros((MLA_Q_LORA, MLA_HEADS, pad), F32)
    wq = jnp.concatenate([uq, zq], axis=-1)
    wqr = jnp.concatenate([jnp.zeros((MLA_Q_LORA, MLA_HEADS, MLA_NOPE_DIM), F32),
                           _swap_halves(uq[..., MLA_NOPE_DIM:]), zq], axis=-1)
    ukv = w_ukv.reshape(MLA_KV_LORA, MLA_HEADS, MLA_NOPE_DIM + HEAD_DIM)
    wkn = jnp.concatenate([ukv[..., :MLA_NOPE_DIM],
                           jnp.zeros((MLA_KV_LORA, MLA_HEADS, MLA_PAD_DIM - MLA_NOPE_DIM), F32)], axis=-1)
    wv = ukv[..., MLA_NOPE_DIM:]
    eye = jnp.eye(MLA_ROPE_DIM, dtype=F32)[:, None, :]
    scat = jnp.concatenate([jnp.zeros((MLA_ROPE_DIM, MLA_HEADS, MLA_NOPE_DIM), F32),
                            jnp.broadcast_to(eye, (MLA_ROPE_DIM, MLA_HEADS, MLA_ROPE_DIM)),
                            jnp.zeros((MLA_ROPE_DIM, MLA_HEADS, pad), F32)], axis=-1)
    scat = jnp.concatenate([scat, scat], axis=0)
    flat = lambda w: w.reshape(w.shape[0], -1).astype(BF16)
    return flat(wq), flat(wqr), flat(wkn), flat(wv), flat(scat)


def _layer(x, w_in, mla_q_norm, mla_w_uq, mla_kv_norm, mla_w_ukv, rwkv_mu, rwkv_w0, rwkv_w2, rwkv_a0, rwkv_a2,
           rwkv_g2, rwkv_k_k, rwkv_k_a, rwkv_r_k, rwkv_gn_g, rwkv_gn_b, w_o, ln1_g, ln1_b,
           ffn_w_up, ffn_conv_w, ffn_conv_b, ffn_w_down, ln2_g, ln2_b, *, alpha, rope):
    b, s, d = x.shape
    t = b * s
    x2d = x.reshape(t, d)
    row_tile = min(512, s)
    attn_blk = min(256, s)

    c0 = SB_COLS
    c1 = c0 + MLA_Q_LORA
    c2 = c1 + MLA_KV_LORA
    c3 = c2 + MLA_ROPE_DIM
    w_kr = w_in[:, c2:c3]
    wkr2 = jnp.concatenate([w_kr, _swap_halves(w_kr)], axis=-1)
    bf = lambda w: w.astype(BF16)
    q_sb, k_sb, v_sb, cq, ckv, kr, hr = _inproj(
        x2d, bf(w_in[:, :c0]), bf(w_in[:, c0:c1]), bf(w_in[:, c1:c2]), bf(wkr2), bf(w_in[:, c3:]), row_tile)

    o_sb = _sb_attention(q_sb.reshape(b, s, -1), k_sb.reshape(b, s, -1), v_sb.reshape(b, s, -1),
                         attn_blk, attn_blk)

    wq, wqr, wkn, wv, scat = _mla_weights(mla_w_uq, mla_w_ukv)
    qcos, qsin, ktab = rope
    q_m, k_m, v_m = _mla_prep(cq, ckv, kr, mla_q_norm[None, :], mla_kv_norm[None, :], wq, wqr, wkn, wv, scat,
                              qcos, qsin, ktab, s, row_tile)
    o_mla = _mla_attention(q_m.reshape(b, s, -1), k_m.reshape(b, s, -1), v_m.reshape(b, s, -1), attn_blk)

    zl = jnp.zeros((RWKV_DECAY_LORA, RWKV_WIDTH), F32)
    wlora = jnp.concatenate([jnp.concatenate([rwkv_w2, zl], axis=1),
                             jnp.concatenate([zl, rwkv_a2], axis=1)], axis=0)
    r, w, k, v, a, g = _rwkv_prep(hr.reshape(b, s, -1), rwkv_mu[None, :], rwkv_w0[None, :], rwkv_a0[None, :],
                                  bf(wlora), bf(rwkv_g2), row_tile)
    y = _rwkv_chunked(r, w, k, v, a, g, rwkv_k_k[None, :], rwkv_k_a[None, :], rwkv_r_k.reshape(1, -1),
                      rwkv_gn_g[None, :], rwkv_gn_b[None, :], RWKV_CHUNK, min(4, s // RWKV_CHUNK))

    x1 = _outproj(x2d, o_sb.reshape(t, -1), o_mla.reshape(t, -1), y.reshape(t, -1),
                  bf(w_o[:SB_WIDTH]), bf(w_o[SB_WIDTH:SB_WIDTH + MLA_WIDTH]), bf(w_o[SB_WIDTH + MLA_WIDTH:]),
                  ln1_g[None, :], ln1_b[None, :], alpha, row_tile)

    d_ff = ffn_w_down.shape[0]
    return _ffn(x1.reshape(b, s, d), bf(ffn_w_up[:, :d_ff]), bf(ffn_w_up[:, d_ff:]), ffn_conv_w,
                ffn_conv_b[None, :], bf(ffn_w_down), ln2_g[None, :], ln2_b[None, :], alpha, min(256, s), 256)


def kernel(x, w_in, mla_q_norm, mla_w_uq, mla_kv_norm, mla_w_ukv, rwkv_mu, rwkv_w0, rwkv_w2, rwkv_a0, rwkv_a2, rwkv_g2, rwkv_k_k, rwkv_k_a, rwkv_r_k, rwkv_gn_g, rwkv_gn_b, w_o, ln1_g, ln1_b, ffn_w_up, ffn_conv_w, ffn_conv_b, ffn_w_down, ln2_g, ln2_b):
    depth = w_in.shape[0]
    alpha = (2 * depth) ** 0.25
    rope = _rope_tables(x.shape[1])
    weights = (w_in, mla_q_norm, mla_w_uq, mla_kv_norm, mla_w_ukv, rwkv_mu, rwkv_w0, rwkv_w2, rwkv_a0, rwkv_a2,
               rwkv_g2, rwkv_k_k, rwkv_k_a, rwkv_r_k, rwkv_gn_g, rwkv_gn_b, w_o, ln1_g, ln1_b,
               ffn_w_up, ffn_conv_w, ffn_conv_b, ffn_w_down, ln2_g, ln2_b)
    for layer in range(depth):
        x = _layer(x, *(w[layer] for w in weights), alpha=alpha, rope=rope)
    return x
```

```python
import functools
import math

import jax
import jax.numpy as jnp
from jax import lax
from jax.experimental import pallas as pl
from jax.experimental.pallas import tpu as pltpu

F32 = jnp.float32
BF16 = jnp.bfloat16

HEAD_DIM = 64
SB_HEADS = 4
MLA_HEADS = 4
RWKV_HEADS = 8
SB_WIDTH = SB_HEADS * HEAD_DIM
MLA_WIDTH = MLA_HEADS * HEAD_DIM
RWKV_WIDTH = RWKV_HEADS * HEAD_DIM
MLA_Q_LORA = 192
MLA_KV_LORA = 128
MLA_NOPE_DIM = 64
MLA_ROPE_DIM = 32
MLA_PAD_DIM = 128
ROPE_THETA = 10000.0
RWKV_DECAY_LORA = 64
RWKV_AAA_LORA = 64
RWKV_GATE_LORA = 128
RWKV_GN_EPS = 64e-5
RWKV_CHUNK = 64
SB_COLS = 3 * SB_WIDTH
MLA_COLS = MLA_Q_LORA + MLA_KV_LORA + MLA_ROPE_DIM
RWKV_COLS = 3 * RWKV_WIDTH + RWKV_DECAY_LORA + RWKV_AAA_LORA + RWKV_GATE_LORA
CONV_WIDTH = 3
LN_EPS = 1e-5
RMS_EPS = 1e-6

LANES = 128
SUBLANES = 8
VMEM_LIMIT = 48 * 1024 * 1024


def _dot(a, b):
    return jnp.dot(a, b, preferred_element_type=F32)


def _dot_nt(a, b):
    return lax.dot_general(a, b, (((1,), (1,)), ((), ())), preferred_element_type=F32)


def _dot_tn(a, b):
    return lax.dot_general(a, b, (((0,), (0,)), ((), ())), preferred_element_type=F32)


def _split_bf16(x, terms):
    out = []
    rem = x
    for _ in range(terms):
        part = rem.astype(BF16)
        out.append(part)
        rem = rem - part.astype(F32)
    return out


def _params(sem):
    return pltpu.CompilerParams(dimension_semantics=sem, vmem_limit_bytes=VMEM_LIMIT)


def _inproj_kernel(x_ref, wsb_ref, wcq_ref, wckv_ref, wkr_ref, wrw_ref,
                   q_ref, k_ref, v_ref, cq_ref, ckv_ref, kr_ref, hr_ref):
    xb = x_ref[...].astype(BF16)
    sb = _dot(xb, wsb_ref[...])
    q_ref[...] = sb[:, :SB_WIDTH].astype(BF16)
    k_ref[...] = sb[:, SB_WIDTH:2 * SB_WIDTH].astype(BF16)
    v_ref[...] = sb[:, 2 * SB_WIDTH:].astype(BF16)
    cq_ref[...] = _dot(xb, wcq_ref[...])
    ckv_ref[...] = _dot(xb, wckv_ref[...])
    kr_ref[...] = _dot(xb, wkr_ref[...])
    hr_ref[...] = _dot(xb, wrw_ref[...])


def _inproj(x2d, wsb, wcq, wckv, wkr, wrw, tm):
    t, d = x2d.shape
    row = lambda n: pl.BlockSpec((tm, n), lambda i: (i, 0))
    full = lambda w: pl.BlockSpec(w.shape, lambda i: (0, 0))
    widths = (SB_WIDTH, SB_WIDTH, SB_WIDTH, MLA_Q_LORA, MLA_KV_LORA, 2 * MLA_ROPE_DIM, RWKV_COLS)
    dtypes = (BF16, BF16, BF16, F32, F32, F32, F32)
    return pl.pallas_call(
        _inproj_kernel,
        grid=(t // tm,),
        in_specs=[row(d), full(wsb), full(wcq), full(wckv), full(wkr), full(wrw)],
        out_specs=[row(n) for n in widths],
        out_shape=[jax.ShapeDtypeStruct((t, n), dt) for n, dt in zip(widths, dtypes)],
        compiler_params=_params(("parallel",)),
        name="inproj",
    )(x2d, wsb, wcq, wckv, wkr, wrw)


def _sb_attn_kernel(q_ref, k_ref, v_ref, o_ref, *, bq, bk, scale):
    i = pl.program_id(2)
    ratio = bk // bq
    q = q_ref[0]
    lane = lax.broadcasted_iota(jnp.int32, (bq, LANES), 1)
    urow = lax.broadcasted_iota(jnp.int32, (bk, bk), 0)
    ucol = lax.broadcasted_iota(jnp.int32, (bk, bk), 1)
    upper = jnp.where(urow > ucol, 1.0, 0.0).astype(BF16)
    row = lax.broadcasted_iota(jnp.int32, (bq, bk), 0) + (i % ratio) * bq
    col = lax.broadcasted_iota(jnp.int32, (bq, bk), 1)
    past_diag = col < row
    head_masks = [lane < HEAD_DIM, lane >= HEAD_DIM]
    assert math.frexp(scale)[0] == 0.5
    neg_q = q * jnp.asarray(-scale, q.dtype)
    qh = [jnp.where(m, neg_q, jnp.zeros_like(q)) for m in head_masks]

    def blocks(js, diags, carry):
        chains = [(n, h) for n in range(len(js)) for h in range(2)]
        starts = [pl.multiple_of(j * bk, bk) for j in js]
        kb = [k_ref[0, pl.ds(st, bk), :] for st in starts]
        vb = [v_ref[0, pl.ds(st, bk), :] for st in starts]
        u = {ch: _dot_nt(qh[ch[1]], kb[ch[0]]) for ch in chains}
        log_keep, split = {}, {}
        for ch in chains:
            soft = jnp.log(1.0 + jnp.exp(-jnp.abs(u[ch])))
            lk = jnp.minimum(u[ch], 0.0) - soft
            if diags[ch[0]]:
                lk = jnp.where(past_diag, lk, 0.0)
            log_keep[ch] = lk
            split[ch] = _split_bf16(lk, 2)
        later = {ch: _dot(split[ch][0], upper) + _dot(split[ch][1], upper) for ch in chains}
        run = [carry[1], carry[3]]
        w = {}
        for ch in chains:
            n, h = ch
            logw = (log_keep[ch] - u[ch]) + later[ch] + run[h]
            wch = jnp.exp(logw)
            if diags[n]:
                wch = jnp.where(past_diag, wch, 0.0)
            w[ch] = wch.astype(BF16)
            run[h] = run[h] + jnp.sum(log_keep[ch], axis=1, keepdims=True)
        acc = [carry[0], carry[2]]
        for ch in chains:
            acc[ch[1]] = acc[ch[1]] + _dot(w[ch], vb[ch[0]])
        return (acc[0], run[0], acc[1], run[1])

    zero_acc = jnp.zeros((bq, LANES), F32)
    zero_run = jnp.zeros((bq, 1), F32)
    init = (zero_acc, zero_run, zero_acc, zero_run)
    jd = i // ratio
    carry = lax.cond(jd % 2 == 1,
                     lambda c: blocks([jd, jd - 1], [True, False], c),
                     lambda c: blocks([jd], [True], c), init)
    rest = jd - jd % 2

    def two_blocks(s, c):
        j = rest - 1 - 2 * s
        return blocks([j, j - 1], [False, False], c)

    carry = lax.fori_loop(0, rest // 2, two_blocks, carry)
    o_ref[0] = jnp.where(head_masks[0], carry[0], carry[2]).astype(o_ref.dtype)


def _sb_attention(q, k, v, bq, bk):
    b, s, w = q.shape
    pairs = w // LANES
    kern = functools.partial(_sb_attn_kernel, bq=bq, bk=bk, scale=HEAD_DIM ** -0.5)
    return pl.pallas_call(
        kern,
        grid=(b, pairs, s // bq),
        in_specs=[pl.BlockSpec((1, bq, LANES), lambda bi, p, i: (bi, i, p)),
                  pl.BlockSpec((1, s, LANES), lambda bi, p, i: (bi, 0, p)),
                  pl.BlockSpec((1, s, LANES), lambda bi, p, i: (bi, 0, p))],
        out_specs=pl.BlockSpec((1, bq, LANES), lambda bi, p, i: (bi, i, p)),
        out_shape=jax.ShapeDtypeStruct((b, s, w), BF16),
        compiler_params=_params(("parallel", "parallel", "arbitrary")),
        name="sb_attn",
    )(q, k, v)


def _mla_prep_kernel(cq_ref, ckv_ref, kr_ref, qn_ref, kvn_ref, wq_ref, wqr_ref, wkn_ref, wv_ref, scat_ref,
                     qcos_ref, qsin_ref, ktab_ref, q_out, k_out, v_out):
    cq = cq_ref[...]
    cq = cq * lax.rsqrt(jnp.mean(jnp.square(cq), axis=-1, keepdims=True) + RMS_EPS) * qn_ref[...]
    cqb = cq.astype(BF16)
    q = _dot(cqb, wq_ref[...]) * qcos_ref[...] + _dot(cqb, wqr_ref[...]) * qsin_ref[...]
    q_out[...] = q.astype(BF16)
    ckv = ckv_ref[...]
    ckv = ckv * lax.rsqrt(jnp.mean(jnp.square(ckv), axis=-1, keepdims=True) + RMS_EPS) * kvn_ref[...]
    ckvb = ckv.astype(BF16)
    v_out[...] = _dot(ckvb, wv_ref[...]).astype(BF16)
    prod = kr_ref[...] * ktab_ref[...]
    hi, lo = _split_bf16(prod, 2)
    rope = _dot(hi, scat_ref[...]) + _dot(lo, scat_ref[...])
    k_out[...] = (_dot(ckvb, wkn_ref[...]) + rope).astype(BF16)


def _mla_prep(cq, ckv, kr, qn, kvn, wq, wqr, wkn, wv, scat, qcos, qsin, ktab, s, tm):
    t = cq.shape[0]
    nj = s // tm
    row = lambda n: pl.BlockSpec((tm, n), lambda i: (i, 0))
    full = lambda w: pl.BlockSpec(w.shape, lambda i: (0, 0))
    pos = lambda n: pl.BlockSpec((tm, n), lambda i: (i % nj, 0))
    qk_w = MLA_HEADS * MLA_PAD_DIM
    return pl.pallas_call(
        _mla_prep_kernel,
        grid=(t // tm,),
        in_specs=[row(MLA_Q_LORA), row(MLA_KV_LORA), row(2 * MLA_ROPE_DIM), full(qn), full(kvn), full(wq),
                  full(wqr), full(wkn), full(wv), full(scat), pos(qk_w), pos(qk_w), pos(2 * MLA_ROPE_DIM)],
        out_specs=[row(qk_w), row(qk_w), row(MLA_WIDTH)],
        out_shape=[jax.ShapeDtypeStruct((t, qk_w), BF16), jax.ShapeDtypeStruct((t, qk_w), BF16),
                   jax.ShapeDtypeStruct((t, MLA_WIDTH), BF16)],
        compiler_params=_params(("parallel",)),
        name="mla_prep",
    )(cq, ckv, kr, qn, kvn, wq, wqr, wkn, wv, scat, qcos, qsin, ktab)


def _mla_attn_kernel(q_ref, k_ref, v_ref, o_ref, *, blk, scale):
    i = pl.program_id(2)
    q = q_ref[0]
    qh = [q[:, :MLA_PAD_DIM], q[:, MLA_PAD_DIM:]]
    row = lax.broadcasted_iota(jnp.int32, (blk, blk), 0)
    col = lax.broadcasted_iota(jnp.int32, (blk, blk), 1)
    causal_diag = col <= row
    lane = lax.broadcasted_iota(jnp.int32, (blk, LANES), 1)

    def blocks(js, diags, carry):
        starts = [pl.multiple_of(j * blk, blk) for j in js]
        kb = [k_ref[0, pl.ds(st, blk), :] for st in starts]
        vb = [v_ref[0, pl.ds(st, blk), :] for st in starts]
        nb = range(len(js))
        sc = [[_dot_nt(qh[h], kb[n][:, h * MLA_PAD_DIM:(h + 1) * MLA_PAD_DIM]) for n in nb] for h in range(2)]
        new = []
        for h in range(2):
            m, l, acc = carry[3 * h:3 * h + 3]
            s_h = [jnp.where(causal_diag, sc[h][n], -jnp.inf) if diags[n] else sc[h][n] for n in nb]
            m_new = m
            for n in nb:
                m_new = jnp.maximum(m_new, jnp.max(s_h[n], axis=1, keepdims=True))
            alpha = jnp.exp2((m - m_new) * exp2_scale)
            p = [jnp.exp2((s_h[n] - m_new) * exp2_scale) for n in nb]
            l = alpha * l
            acc = alpha * acc
            for n in nb:
                l = l + jnp.sum(p[n], axis=1, keepdims=True)
            for n in nb:
                acc = acc + _dot(p[n].astype(BF16), vb[n])
            new += [m_new, l, acc]
        return tuple(new)

    exp2_scale = scale * math.log2(math.e)
    init = (jnp.full((blk, 1), -jnp.inf, F32), jnp.zeros((blk, 1), F32), jnp.zeros((blk, LANES), F32)) * 2
    carry = lax.fori_loop(0, i // 2, lambda s, c: blocks([2 * s, 2 * s + 1], [False, False], c), init)
    carry = lax.cond(i % 2 == 1,
                     lambda c: blocks([i - 1, i], [False, True], c),
                     lambda c: blocks([i], [True], c), carry)
    out = jnp.where(lane < HEAD_DIM, carry[2] / carry[1], carry[5] / carry[4])
    o_ref[0] = out.astype(o_ref.dtype)


def _mla_attention(q, k, v, blk):
    b, s, _ = q.shape
    pairs = MLA_HEADS // 2
    kern = functools.partial(_mla_attn_kernel, blk=blk, scale=(MLA_NOPE_DIM + MLA_ROPE_DIM) ** -0.5)
    return pl.pallas_call(
        kern,
        grid=(b, pairs, s // blk),
        in_specs=[pl.BlockSpec((1, blk, 2 * MLA_PAD_DIM), lambda bi, p, i: (bi, i, p)),
                  pl.BlockSpec((1, s, 2 * MLA_PAD_DIM), lambda bi, p, i: (bi, 0, p)),
                  pl.BlockSpec((1, s, LANES), lambda bi, p, i: (bi, 0, p))],
        out_specs=pl.BlockSpec((1, blk, LANES), lambda bi, p, i: (bi, i, p)),
        out_shape=jax.ShapeDtypeStruct((b, s, MLA_WIDTH), BF16),
        compiler_params=_params(("parallel", "parallel", "arbitrary")),
        name="mla_attn",
    )(q, k, v)


def _rwkv_prep_kernel(h_ref, mu_ref, w0_ref, a0_ref, wlora_ref, g2_ref,
                      r_out, w_out, k_out, v_out, a_out, g_out, shift_ref):
    tm = h_ref.shape[1]
    c = RWKV_WIDTH

    @pl.when(pl.program_id(1) == 0)
    def _():
        shift_ref[0:SUBLANES, :] = jnp.zeros((SUBLANES, RWKV_COLS), F32)

    h = h_ref[0]
    shift_ref[SUBLANES:, :] = h
    prev = shift_ref[pl.ds(SUBLANES - 1, tm), :]
    shift_ref[0:SUBLANES, :] = h[tm - SUBLANES:, :]
    p = h + (prev - h) * mu_ref[...]
    r_out[0] = p[:, :c]
    k_out[0] = p[:, c:2 * c]
    v_out[0] = p[:, 2 * c:3 * c]
    lora_in = p[:, 3 * c:3 * c + LANES]
    lane = lax.broadcasted_iota(jnp.int32, lora_in.shape, 1)
    lora_in = jnp.where(lane < RWKV_DECAY_LORA, jnp.tanh(lora_in), lora_in)
    lora = _dot(lora_in.astype(BF16), wlora_ref[...])
    w = -jax.nn.softplus(-(w0_ref[...] + lora[:, :c])) - 0.5
    w_out[0] = -jnp.exp(w)
    a_out[0] = jax.nn.sigmoid(a0_ref[...] + lora[:, c:])
    gate_in = jax.nn.sigmoid(p[:, 3 * c + LANES:])
    g_out[0] = _dot(gate_in.astype(BF16), g2_ref[...])


def _rwkv_prep(hr, mu, w0, a0, wlora, g2, tm):
    b, s, _ = hr.shape
    full = lambda w: pl.BlockSpec(w.shape, lambda bi, j: (0, 0))
    out = pl.BlockSpec((1, tm, RWKV_WIDTH), lambda bi, j: (bi, j, 0))
    return pl.pallas_call(
        _rwkv_prep_kernel,
        grid=(b, s // tm),
        in_specs=[pl.BlockSpec((1, tm, RWKV_COLS), lambda bi, j: (bi, j, 0)),
                  full(mu), full(w0), full(a0), full(wlora), full(g2)],
        out_specs=[out] * 6,
        out_shape=[jax.ShapeDtypeStruct((b, s, RWKV_WIDTH), F32)] * 6,
        scratch_shapes=[pltpu.VMEM((tm + SUBLANES, RWKV_COLS), F32)],
        compiler_params=_params(("arbitrary", "arbitrary")),
        name="rwkv_prep",
    )(hr, mu, w0, a0, wlora, g2)


def _mm(a, b, dot=_dot):
    return dot(a.astype(BF16), b.astype(BF16))


def _rwkv_chunk_kernel(r_ref, lw_ref, k_ref, v_ref, a_ref, g_ref, kk_ref, ka_ref, rk_ref, gng_ref, gnb_ref,
                       y_ref, state_ref, *, chunk, n_sub, group):
    c = chunk
    n = HEAD_DIM
    pairs = RWKV_HEADS // 2

    @pl.when(pl.program_id(1) == 0)
    def _():
        state_ref[...] = jnp.zeros(state_ref.shape, F32)

    rowi = lax.broadcasted_iota(jnp.int32, (c, c), 0)
    coli = lax.broadcasted_iota(jnp.int32, (c, c), 1)
    tri_incl = jnp.where(rowi >= coli, 1.0, 0.0).astype(BF16)
    assert 2 * c == LANES and n == c
    t_row = lax.broadcasted_iota(jnp.int32, (c, LANES), 0)
    lane = lax.broadcasted_iota(jnp.int32, (c, LANES), 1)
    t_col = jnp.where(lane < c, lane, lane - c)
    strict = t_col < t_row
    incl = t_col <= t_row
    eye2 = jnp.where(t_col == t_row, 1.0, 0.0).astype(F32)
    m0 = lane < n
    own = jnp.concatenate([m0, jnp.logical_not(m0)], axis=0)
    lr = lax.broadcasted_iota(jnp.int32, (LANES, LANES), 0)
    lc = lax.broadcasted_iota(jnp.int32, (LANES, LANES), 1)
    blockdiag = (lr >= n) == (lc >= n)

    def stack(x):
        return jnp.where(own, jnp.concatenate([x, x], axis=0), 0.0)

    def head_sum(x):
        s0 = jnp.sum(jnp.where(m0, x, 0.0), axis=1, keepdims=True)
        s1 = jnp.sum(jnp.where(m0, 0.0, x), axis=1, keepdims=True)
        return jnp.where(m0, s0, s1)

    P = range(pairs)
    ln = [slice(p * LANES, (p + 1) * LANES) for p in P]
    for first in range(0, n_sub, group):
        subs = range(first, first + group)
        chains = [(sub, p) for sub in subs for p in P]
        g_end, KK, A, B, R, K, V, bonus = {}, {}, {}, {}, {}, {}, {}, {}
        for sub in subs:
            rows = pl.ds(sub * c, c)
            r = r_ref[0, rows, :]
            logw = lw_ref[0, rows, :]
            k = k_ref[0, rows, :]
            v = v_ref[0, rows, :]
            a = a_ref[0, rows, :]
            kk = k * kk_ref[...]
            kp = k * (1.0 + (a - 1.0) * ka_ref[...])
            bon = r * kp * rk_ref[...]
            l3 = _split_bf16(logw, 3)
            cum = _dot(tri_incl, l3[0]) + _dot(tri_incl, l3[1]) + _dot(tri_incl, l3[2])
            g_in = jnp.exp(cum)
            g_ex = jnp.exp(cum - logw)
            g_inv = jnp.exp(-cum)
            rt = r * g_in
            kt = kp * g_inv
            ag = a * g_inv
            g_end[sub] = g_in[c - 1:c, :]
            for p in P:
                ch = (sub, p)
                x = kk[:, ln[p]]
                x = x * lax.rsqrt(jnp.maximum(head_sum(jnp.square(x)), 1e-12))
                A[ch] = -x * g_ex[:, ln[p]]
                B[ch] = x * ag[:, ln[p]]
                R[ch] = rt[:, ln[p]]
                K[ch] = kt[:, ln[p]]
                V[ch] = v[:, ln[p]]
                bonus[ch] = head_sum(bon[:, ln[p]]) * V[ch]
        big = {ch: _mm(jnp.concatenate([A[ch], R[ch]], axis=0),
                       jnp.concatenate([stack(B[ch]), stack(K[ch])], axis=0), _dot_nt) for ch in chains}
        a_ab = {ch: jnp.where(strict, big[ch][:c, :LANES], 0.0) for ch in chains}
        a_ak = {ch: jnp.where(strict, big[ch][:c, LANES:], 0.0) for ch in chains}
        a_rb = {ch: jnp.where(incl, big[ch][c:, :LANES], 0.0) for ch in chains}
        a_rk = {ch: jnp.where(incl, big[ch][c:, LANES:], 0.0) for ch in chains}
        T = {ch: eye2 + a_ab[ch] for ch in chains}
        Lp = {ch: _mm(a_ab[ch], stack(a_ab[ch])) for ch in chains}
        for _ in range(int(math.log2(c)) - 2):
            X = {ch: _mm(jnp.concatenate([T[ch], Lp[ch]], axis=0), stack(Lp[ch])) for ch in chains}
            T = {ch: T[ch] + X[ch][:c] for ch in chains}
            Lp = {ch: X[ch][c:] for ch in chains}
        T = {ch: T[ch] + _mm(T[ch], stack(Lp[ch])) for ch in chains}
        akv = {ch: _mm(a_ak[ch], stack(V[ch])) for ch in chains}
        W = {ch: _mm(T[ch], jnp.concatenate([stack(A[ch]), stack(akv[ch])], axis=1)) for ch in chains}
        W1 = {ch: W[ch][:, :LANES] for ch in chains}
        W2 = {ch: W[ch][:, LANES:] for ch in chains}
        Pm = {ch: jnp.where(blockdiag, _mm(W1[ch], B[ch], _dot_tn), 0.0) for ch in chains}
        Nt = {ch: jnp.where(blockdiag, _mm(jnp.concatenate([W2[ch], V[ch]], axis=0),
                                           jnp.concatenate([B[ch], K[ch]], axis=0), _dot_tn), 0.0) for ch in chains}
        Qe = {ch: R[ch] + _mm(a_rb[ch], stack(W1[ch])) for ch in chains}
        Yl = {ch: _mm(jnp.concatenate([a_rb[ch], a_rk[ch]], axis=1),
                      jnp.concatenate([stack(W2[ch]), stack(V[ch])], axis=0)) for ch in chains}
        for sub in subs:
            cs = [(sub, p) for p in P]
            H0T = [state_ref[p] for p in P]
            HP = [_mm(H0T[p], Pm[ch]) for p, ch in enumerate(cs)]
            Y = [_mm(Qe[ch], H0T[p], _dot_nt) + Yl[ch] for p, ch in enumerate(cs)]
            for p, ch in enumerate(cs):
                state_ref[p] = (H0T[p] + HP[p] + Nt[ch]) * g_end[sub][:, ln[p]]
            ys = []
            for p, ch in enumerate(cs):
                yc = Y[p] - head_sum(Y[p]) * (1.0 / n)
                var = head_sum(jnp.square(yc)) * (1.0 / n)
                ys.append((yc * lax.rsqrt(var + RWKV_GN_EPS), bonus[ch]))
            yn = jnp.concatenate([t[0] for t in ys], axis=1)
            bo = jnp.concatenate([t[1] for t in ys], axis=1)
            rows = pl.ds(sub * c, c)
            y_ref[0, rows, :] = (yn * gng_ref[...] + gnb_ref[...] + bo) * g_ref[0, rows, :]


def _rwkv_chunked(r, logw, k, v, a, g, k_k, k_a, r_k, gn_g, gn_b, chunk, n_sub, group):
    b, s, width = r.shape
    tm = chunk * n_sub
    seq = pl.BlockSpec((1, tm, width), lambda bi, j: (bi, j, 0))
    par = pl.BlockSpec((1, width), lambda bi, j: (0, 0))
    kern = functools.partial(_rwkv_chunk_kernel, chunk=chunk, n_sub=n_sub, group=group)
    return pl.pallas_call(
        kern,
        grid=(b, s // tm),
        in_specs=[seq] * 6 + [par] * 5,
        out_specs=seq,
        out_shape=jax.ShapeDtypeStruct((b, s, width), F32),
        scratch_shapes=[pltpu.VMEM((RWKV_HEADS // 2, LANES, LANES), F32)],
        compiler_params=_params(("arbitrary", "arbitrary")),
        name="rwkv_chunk",
    )(r, logw, k, v, a, g, k_k, k_a, r_k, gn_g, gn_b)


def _layer_norm(x, g, b):
    mu = jnp.mean(x, axis=-1, keepdims=True)
    xc = x - mu
    var = jnp.mean(jnp.square(xc), axis=-1, keepdims=True)
    return xc * lax.rsqrt(var + LN_EPS) * g + b


def _outproj_kernel(x_ref, sb_ref, mla_ref, y_ref, wsb_ref, wmla_ref, wrw_ref, lng_ref, lnb_ref,
                    o_ref, *, alpha):
    mix = _dot(sb_ref[...], wsb_ref[...]) + _dot(mla_ref[...], wmla_ref[...])
    mix = mix + _dot(y_ref[...].astype(BF16), wrw_ref[...])
    o_ref[...] = _layer_norm(alpha * x_ref[...] + mix, lng_ref[...], lnb_ref[...])


def _outproj(x2d, o_sb, o_mla, y, wsb, wmla, wrw, lng, lnb, alpha, tm):
    t, d = x2d.shape
    row = lambda n: pl.BlockSpec((tm, n), lambda i: (i, 0))
    full = lambda w: pl.BlockSpec(w.shape, lambda i: (0, 0))
    return pl.pallas_call(
        functools.partial(_outproj_kernel, alpha=alpha),
        grid=(t // tm,),
        in_specs=[row(d), row(SB_WIDTH), row(MLA_WIDTH), row(RWKV_WIDTH),
                  full(wsb), full(wmla), full(wrw), full(lng), full(lnb)],
        out_specs=row(d),
        out_shape=jax.ShapeDtypeStruct((t, d), F32),
        compiler_params=_params(("parallel",)),
        name="outproj_ln",
    )(x2d, o_sb, o_mla, y, wsb, wmla, wrw, lng, lnb)


def _ffn_kernel(x_ref, wact_ref, wgate_ref, convw_ref, convb_ref, wdown_ref, lng_ref, lnb_ref,
                o_ref, hid_ref, shift_ref, carry_ref, *, alpha, fc):
    tm = x_ref.shape[1]
    d_ff = wact_ref.shape[1]

    @pl.when(pl.program_id(1) == 0)
    def _():
        carry_ref[...] = jnp.zeros(carry_ref.shape, F32)

    x = x_ref[0]
    xb = x.astype(BF16)
    for c in range(d_ff // fc):
        cols = slice(c * fc, (c + 1) * fc)
        u = _dot(xb, wact_ref[:, cols])
        shift_ref[0:SUBLANES, :] = carry_ref[:, cols]
        shift_ref[SUBLANES:, :] = u
        carry_ref[:, cols] = u[tm - SUBLANES:, :]
        prev1 = shift_ref[pl.ds(SUBLANES - 1, tm), :]
        prev2 = shift_ref[pl.ds(SUBLANES - 2, tm), :]
        conv = (prev2 * convw_ref[0:1, cols] + prev1 * convw_ref[1:2, cols] + u * convw_ref[2:3, cols]
                + convb_ref[:, cols])
        gate = _dot(xb, wgate_ref[:, cols])
        gelu = 0.5 * conv * (1.0 + lax.erf(conv * math.sqrt(0.5)))
        hid_ref[:, cols] = (gelu * gate).astype(BF16)
    down = _dot(hid_ref[...], wdown_ref[...])
    o_ref[0] = _layer_norm(alpha * x + down, lng_ref[...], lnb_ref[...])


def _ffn(x, wact, wgate, convw, convb, wdown, lng, lnb, alpha, tm, fc):
    b, s, d = x.shape
    d_ff = wact.shape[1]
    const = lambda w: pl.BlockSpec(w.shape, lambda bi, j: (0, 0), pipeline_mode=pl.Buffered(1))
    tile = pl.BlockSpec((1, tm, d), lambda bi, j: (bi, j, 0))
    return pl.pallas_call(
        functools.partial(_ffn_kernel, alpha=alpha, fc=fc),
        grid=(b, s // tm),
        in_specs=[tile, const(wact), const(wgate), const(convw), const(convb), const(wdown),
                  const(lng), const(lnb)],
        out_specs=tile,
        out_shape=jax.ShapeDtypeStruct((b, s, d), F32),
        scratch_shapes=[pltpu.VMEM((tm, d_ff), BF16),
                        pltpu.VMEM((tm + SUBLANES, fc), F32),
                        pltpu.VMEM((SUBLANES, d_ff), F32)],
        compiler_params=_params(("arbitrary", "arbitrary")),
        name="conv_ffn",
    )(x, wact, wgate, convw, convb, wdown, lng, lnb)


def _rope_tables(s):
    half = MLA_ROPE_DIM // 2
    inv_freq = 1.0 / (ROPE_THETA ** (jnp.arange(0, MLA_ROPE_DIM, 2, dtype=F32) / MLA_ROPE_DIM))
    ang = jnp.arange(s, dtype=F32)[:, None] * inv_freq[None, :]
    cos, sin = jnp.cos(ang), jnp.sin(ang)
    cos2 = jnp.concatenate([cos, cos], axis=-1)
    sin2 = jnp.concatenate([-sin, sin], axis=-1)
    pad = jnp.zeros((s, MLA_PAD_DIM - MLA_NOPE_DIM - MLA_ROPE_DIM), F32)
    qcos = jnp.tile(jnp.concatenate([jnp.ones((s, MLA_NOPE_DIM), F32), cos2, pad], axis=-1), (1, MLA_HEADS))
    qsin = jnp.tile(jnp.concatenate([jnp.zeros((s, MLA_NOPE_DIM), F32), sin2, pad], axis=-1), (1, MLA_HEADS))
    ktab = jnp.concatenate([cos2, sin2], axis=-1)
    del half
    return qcos, qsin, ktab


def _swap_halves(w):
    half = w.shape[-1] // 2
    return jnp.concatenate([w[..., half:], w[..., :half]], axis=-1)


def _mla_weights(w_uq, w_ukv):
    qd = MLA_NOPE_DIM + MLA_ROPE_DIM
    pad = MLA_PAD_DIM - qd
    uq = w_uq.reshape(MLA_Q_LORA, MLA_HEADS, qd)
    zq = jnp.zeros((MLA_Q_LORA, MLA_HEADS, pad), F32)
    wq = jnp.concatenate([uq, zq], axis=-1)
    wqr = jnp.concatenate([jnp.zeros((MLA_Q_LORA, MLA_HEADS, MLA_NOPE_DIM), F32),
                           _swap_halves(uq[..., MLA_NOPE_DIM:]), zq], axis=-1)
    ukv = w_ukv.reshape(MLA_KV_LORA, MLA_HEADS, MLA_NOPE_DIM + HEAD_DIM)
    wkn = jnp.concatenate([ukv[..., :MLA_NOPE_DIM],
                           jnp.zeros((MLA_KV_LORA, MLA_HEADS, MLA_PAD_DIM - MLA_NOPE_DIM), F32)], axis=-1)
    wv = ukv[..., MLA_NOPE_DIM:]
    eye = jnp.eye(MLA_ROPE_DIM, dtype=F32)[:, None, :]
    scat = jnp.concatenate([jnp.zeros((MLA_ROPE_DIM, MLA_HEADS, MLA_NOPE_DIM), F32),
                            jnp.broadcast_to(eye, (MLA_ROPE_DIM, MLA_HEADS, MLA_ROPE_DIM)),
                            jnp.zeros((MLA_ROPE_DIM, MLA_HEADS, pad), F32)], axis=-1)
    scat = jnp.concatenate([scat, scat], axis=0)
    flat = lambda w: w.reshape(w.shape[0], -1).astype(BF16)
    return flat(wq), flat(wqr), flat(wkn), flat(wv), flat(scat)


def _layer(x, w_in, mla_q_norm, mla_w_uq, mla_kv_norm, mla_w_ukv, rwkv_mu, rwkv_w0, rwkv_w2, rwkv_a0, rwkv_a2,
           rwkv_g2, rwkv_k_k, rwkv_k_a, rwkv_r_k, rwkv_gn_g, rwkv_gn_b, w_o, ln1_g, ln1_b,
           ffn_w_up, ffn_conv_w, ffn_conv_b, ffn_w_down, ln2_g, ln2_b, *, alpha, rope):
    b, s, d = x.shape
    t = b * s
    x2d = x.reshape(t, d)
    row_tile = min(512, s)
    attn_blk = min(256, s)

    c0 = SB_COLS
    c1 = c0 + MLA_Q_LORA
    c2 = c1 + MLA_KV_LORA
    c3 = c2 + MLA_ROPE_DIM
    w_kr = w_in[:, c2:c3]
    wkr2 = jnp.concatenate([w_kr, _swap_halves(w_kr)], axis=-1)
    bf = lambda w: w.astype(BF16)
    q_sb, k_sb, v_sb, cq, ckv, kr, hr = _inproj(
        x2d, bf(w_in[:, :c0]), bf(w_in[:, c0:c1]), bf(w_in[:, c1:c2]), bf(wkr2), bf(w_in[:, c3:]), row_tile)

    o_sb = _sb_attention(q_sb.reshape(b, s, -1), k_sb.reshape(b, s, -1), v_sb.reshape(b, s, -1),
                         attn_blk, attn_blk)

    wq, wqr, wkn, wv, scat = _mla_weights(mla_w_uq, mla_w_ukv)
    qcos, qsin, ktab = rope
    q_m, k_m, v_m = _mla_prep(cq, ckv, kr, mla_q_norm[None, :], mla_kv_norm[None, :], wq, wqr, wkn, wv, scat,
                              qcos, qsin, ktab, s, row_tile)
    o_mla = _mla_attention(q_m.reshape(b, s, -1), k_m.reshape(b, s, -1), v_m.reshape(b, s, -1), attn_blk)

    zl = jnp.zeros((RWKV_DECAY_LORA, RWKV_WIDTH), F32)
    wlora = jnp.concatenate([jnp.concatenate([rwkv_w2, zl], axis=1),
                             jnp.concatenate([zl, rwkv_a2], axis=1)], axis=0)
    r, w, k, v, a, g = _rwkv_prep(hr.reshape(b, s, -1), rwkv_mu[None, :], rwkv_w0[None, :], rwkv_a0[None, :],
                                  bf(wlora), bf(rwkv_g2), row_tile)
    n_sub = min(8, s // RWKV_CHUNK)
    y = _rwkv_chunked(r, w, k, v, a, g, rwkv_k_k[None, :], rwkv_k_a[None, :], rwkv_r_k.reshape(1, -1),
                      rwkv_gn_g[None, :], rwkv_gn_b[None, :], RWKV_CHUNK, n_sub, n_sub)

    x1 = _outproj(x2d, o_sb.reshape(t, -1), o_mla.reshape(t, -1), y.reshape(t, -1),
                  bf(w_o[:SB_WIDTH]), bf(w_o[SB_WIDTH:SB_WIDTH + MLA_WIDTH]), bf(w_o[SB_WIDTH + MLA_WIDTH:]),
                  ln1_g[None, :], ln1_b[None, :], alpha, row_tile)

    d_ff = ffn_w_down.shape[0]
    return _ffn(x1.reshape(b, s, d), bf(ffn_w_up[:, :d_ff]), bf(ffn_w_up[:, d_ff:]), ffn_conv_w,
                ffn_conv_b[None, :], bf(ffn_w_down), ln2_g[None, :], ln2_b[None, :], alpha, min(256, s), 256)


def kernel(x, w_in, mla_q_norm, mla_w_uq, mla_kv_norm, mla_w_ukv, rwkv_mu, rwkv_w0, rwkv_w2, rwkv_a0, rwkv_a2, rwkv_g2, rwkv_k_k, rwkv_k_a, rwkv_r_k, rwkv_gn_g, rwkv_gn_b, w_o, ln1_g, ln1_b, ffn_w_up, ffn_conv_w, ffn_conv_b, ffn_w_down, ln2_g, ln2_b):
    depth = w_in.shape[0]
    alpha = (2 * depth) ** 0.25
    rope = _rope_tables(x.shape[1])
    weights = (w_in, mla_q_norm, mla_w_uq, mla_kv_norm, mla_w_ukv, rwkv_mu, rwkv_w0, rwkv_w2, rwkv_a0, rwkv_a2,
               rwkv_g2, rwkv_k_k, rwkv_k_a, rwkv_r_k, rwkv_gn_g, rwkv_gn_b, w_o, ln1_g, ln1_b,
               ffn_w_up, ffn_conv_w, ffn_conv_b, ffn_w_down, ln2_g, ln2_b)
    for layer in range(depth):
        x = _layer(x, *(w[layer] for w in weights), alpha=alpha, rope=rope)
    return x
```

```python
import functools
import math

import jax
import jax.numpy as jnp
from jax import lax
from jax.experimental import pallas as pl
from jax.experimental.pallas import tpu as pltpu

F32 = jnp.float32
BF16 = jnp.bfloat16

HEAD_DIM = 64
SB_HEADS = 4
MLA_HEADS = 4
RWKV_HEADS = 8
SB_WIDTH = SB_HEADS * HEAD_DIM
MLA_WIDTH = MLA_HEADS * HEAD_DIM
RWKV_WIDTH = RWKV_HEADS * HEAD_DIM
MLA_Q_LORA = 192
MLA_KV_LORA = 128
MLA_NOPE_DIM = 64
MLA_ROPE_DIM = 32
MLA_PAD_DIM = 128
ROPE_THETA = 10000.0
RWKV_DECAY_LORA = 64
RWKV_AAA_LORA = 64
RWKV_GATE_LORA = 128
RWKV_GN_EPS = 64e-5
FFN_COL_CHUNK = 256
RWKV_CHUNK = 64
SB_COLS = 3 * SB_WIDTH
MLA_COLS = MLA_Q_LORA + MLA_KV_LORA + MLA_ROPE_DIM
RWKV_COLS = 3 * RWKV_WIDTH + RWKV_DECAY_LORA + RWKV_AAA_LORA + RWKV_GATE_LORA
CONV_WIDTH = 3
LN_EPS = 1e-5
RMS_EPS = 1e-6

LANES = 128
SUBLANES = 8
VMEM_LIMIT = 48 * 1024 * 1024


def _dot(a, b):
    return jnp.dot(a, b, preferred_element_type=F32)


def _dot_nt(a, b):
    return lax.dot_general(a, b, (((1,), (1,)), ((), ())), preferred_element_type=F32)


def _dot_tn(a, b):
    return lax.dot_general(a, b, (((0,), (0,)), ((), ())), preferred_element_type=F32)


def _split_bf16(x, terms):
    out = []
    rem = x
    for _ in range(terms):
        part = rem.astype(BF16)
        out.append(part)
        rem = rem - part.astype(F32)
    return out


def _params(sem):
    return pltpu.CompilerParams(dimension_semantics=sem, vmem_limit_bytes=VMEM_LIMIT)


def _rms_norm(x, g):
    return x * lax.rsqrt(jnp.mean(jnp.square(x), axis=-1, keepdims=True) + RMS_EPS) * g


def _inproj_kernel(x_ref, wsb_ref, wcq_ref, wckv_ref, wkr_ref, wrw_ref,
                   qn_ref, kvn_ref, wq_ref, wqr_ref, wkn_ref, wv_ref, scat_ref, qcos_ref, qsin_ref, ktab_ref,
                   mu_ref, w0_ref, a0_ref, wlora_ref, g2_ref,
                   sq_ref, sk_ref, sv_ref, mq_ref, mk_ref, mv_ref,
                   r_out, w_out, k_out, v_out, a_out, g_out, shift_ref):
    tm = x_ref.shape[1]
    c = RWKV_WIDTH
    xb = x_ref[0].astype(BF16)

    @pl.when(pl.program_id(1) == 0)
    def _():
        shift_ref[0:SUBLANES, :] = jnp.zeros((SUBLANES, RWKV_COLS), F32)

    def shifted(cols):
        h = _dot(xb, wrw_ref[:, cols])
        shift_ref[SUBLANES:, cols] = h
        prev = shift_ref[pl.ds(SUBLANES - 1, tm), cols]
        shift_ref[0:SUBLANES, cols] = h[tm - SUBLANES:, :]
        return h + (prev - h) * mu_ref[:, cols]

    p_lora = shifted(slice(3 * c, RWKV_COLS))
    lora_in = p_lora[:, :LANES]
    lane = lax.broadcasted_iota(jnp.int32, lora_in.shape, 1)
    lora_in = jnp.where(lane < RWKV_DECAY_LORA, jnp.tanh(lora_in), lora_in)
    lora = _dot(lora_in.astype(BF16), wlora_ref[...])
    g_out[0] = _dot(jax.nn.sigmoid(p_lora[:, LANES:]).astype(BF16), g2_ref[...])
    w = -jax.nn.softplus(-(w0_ref[...] + lora[:, :c])) - 0.5
    w_out[0] = -jnp.exp(w)
    a_out[0] = jax.nn.sigmoid(a0_ref[...] + lora[:, c:])

    p = shifted(slice(0, 3 * c))
    r_out[0] = p[:, :c]
    k_out[0] = p[:, c:2 * c]
    v_out[0] = p[:, 2 * c:]

    sb = _dot(xb, wsb_ref[...])
    sq_ref[0] = sb[:, :SB_WIDTH].astype(BF16)
    sk_ref[0] = sb[:, SB_WIDTH:2 * SB_WIDTH].astype(BF16)
    sv_ref[0] = sb[:, 2 * SB_WIDTH:].astype(BF16)

    cqb = _rms_norm(_dot(xb, wcq_ref[...]), qn_ref[...]).astype(BF16)
    ckvb = _rms_norm(_dot(xb, wckv_ref[...]), kvn_ref[...]).astype(BF16)
    prod = _dot(xb, wkr_ref[...]) * ktab_ref[...]
    q = _dot(cqb, wq_ref[...]) * qcos_ref[...] + _dot(cqb, wqr_ref[...]) * qsin_ref[...]
    mq_ref[0] = q.astype(BF16)
    mv_ref[0] = _dot(ckvb, wv_ref[...]).astype(BF16)
    hi, lo = _split_bf16(prod, 2)
    rope = _dot(hi, scat_ref[...]) + _dot(lo, scat_ref[...])
    mk_ref[0] = (_dot(ckvb, wkn_ref[...]) + rope).astype(BF16)


def _inproj(x, weights, tables, tm):
    b, s, d = x.shape
    const = lambda w: pl.BlockSpec(w.shape, lambda bi, j: (0, 0), pipeline_mode=pl.Buffered(1))
    tile = lambda n: pl.BlockSpec((1, tm, n), lambda bi, j: (bi, j, 0))
    pos = lambda tab: pl.BlockSpec((tm, tab.shape[1]), lambda bi, j: (j, 0))
    qk_w = MLA_HEADS * MLA_PAD_DIM
    outs = [(SB_WIDTH, BF16)] * 3 + [(qk_w, BF16), (qk_w, BF16), (MLA_WIDTH, BF16)] + [(RWKV_WIDTH, F32)] * 6
    n_mla = 12
    return pl.pallas_call(
        _inproj_kernel,
        grid=(b, s // tm),
        in_specs=([tile(d)] + [const(w) for w in weights[:n_mla]] + [pos(t) for t in tables]
                  + [const(w) for w in weights[n_mla:]]),
        out_specs=[tile(n) for n, _ in outs],
        out_shape=[jax.ShapeDtypeStruct((b, s, n), dt) for n, dt in outs],
        scratch_shapes=[pltpu.VMEM((tm + SUBLANES, RWKV_COLS), F32)],
        compiler_params=_params(("arbitrary", "arbitrary")),
        name="inproj",
    )(x, *weights[:n_mla], *tables, *weights[n_mla:])


def _sb_attn_kernel(q_ref, k_ref, v_ref, o_ref, *, bq, bk, scale):
    i = pl.program_id(2)
    ratio = bk // bq
    q = q_ref[0]
    lane = lax.broadcasted_iota(jnp.int32, (bq, LANES), 1)
    urow = lax.broadcasted_iota(jnp.int32, (bk, bk), 0)
    ucol = lax.broadcasted_iota(jnp.int32, (bk, bk), 1)
    upper = jnp.where(urow > ucol, 1.0, 0.0).astype(BF16)
    row = lax.broadcasted_iota(jnp.int32, (bq, bk), 0) + (i % ratio) * bq
    col = lax.broadcasted_iota(jnp.int32, (bq, bk), 1)
    past_diag = col < row
    head_masks = [lane < HEAD_DIM, lane >= HEAD_DIM]
    assert math.frexp(scale)[0] == 0.5
    neg_q = q * jnp.asarray(-scale, q.dtype)
    qh = [jnp.where(m, neg_q, jnp.zeros_like(q)) for m in head_masks]

    def blocks(js, diags, carry):
        chains = [(n, h) for n in range(len(js)) for h in range(2)]
        starts = [pl.multiple_of(j * bk, bk) for j in js]
        kb = [k_ref[0, pl.ds(st, bk), :] for st in starts]
        vb = [v_ref[0, pl.ds(st, bk), :] for st in starts]
        u = {ch: _dot_nt(qh[ch[1]], kb[ch[0]]) for ch in chains}
        log_keep, split = {}, {}
        for ch in chains:
            neg_abs = lax.bitcast_convert_type(
                lax.bitcast_convert_type(u[ch], jnp.uint32) | jnp.uint32(0x80000000), F32)
            soft = jnp.log(1.0 + jnp.exp(neg_abs))
            lk = jnp.minimum(u[ch], 0.0) - soft
            if diags[ch[0]]:
                lk = jnp.where(past_diag, lk, 0.0)
            log_keep[ch] = lk
            split[ch] = _split_bf16(lk, 2)
        later = {ch: _dot(split[ch][0], upper) + _dot(split[ch][1], upper) for ch in chains}
        run = [carry[1], carry[3]]
        w = {}
        for ch in chains:
            n, h = ch
            logw = (log_keep[ch] - u[ch]) + later[ch] + run[h]
            wch = jnp.exp(logw)
            if diags[n]:
                wch = jnp.where(past_diag, wch, 0.0)
            w[ch] = wch.astype(BF16)
            run[h] = run[h] + jnp.sum(log_keep[ch], axis=1, keepdims=True)
        acc = [carry[0], carry[2]]
        for ch in chains:
            acc[ch[1]] = acc[ch[1]] + _dot(w[ch], vb[ch[0]])
        return (acc[0], run[0], acc[1], run[1])

    zero_acc = jnp.zeros((bq, LANES), F32)
    zero_run = jnp.zeros((bq, 1), F32)
    init = (zero_acc, zero_run, zero_acc, zero_run)
    jd = i // ratio
    carry = lax.cond(jd % 2 == 1,
                     lambda c: blocks([jd, jd - 1], [True, False], c),
                     lambda c: blocks([jd], [True], c), init)
    rest = jd - jd % 2

    def two_blocks(s, c):
        j = rest - 1 - 2 * s
        return blocks([j, j - 1], [False, False], c)

    carry = lax.fori_loop(0, rest // 2, two_blocks, carry)
    o_ref[0] = jnp.where(head_masks[0], carry[0], carry[2]).astype(o_ref.dtype)


def _sb_attention(q, k, v, bq, bk):
    b, s, w = q.shape
    pairs = w // LANES
    kern = functools.partial(_sb_attn_kernel, bq=bq, bk=bk, scale=HEAD_DIM ** -0.5)
    return pl.pallas_call(
        kern,
        grid=(b, pairs, s // bq),
        in_specs=[pl.BlockSpec((1, bq, LANES), lambda bi, p, i: (bi, i, p)),
                  pl.BlockSpec((1, s, LANES), lambda bi, p, i: (bi, 0, p)),
                  pl.BlockSpec((1, s, LANES), lambda bi, p, i: (bi, 0, p))],
        out_specs=pl.BlockSpec((1, bq, LANES), lambda bi, p, i: (bi, i, p)),
        out_shape=jax.ShapeDtypeStruct((b, s, w), BF16),
        compiler_params=_params(("parallel", "parallel", "arbitrary")),
        name="sb_attn",
    )(q, k, v)


def _mla_attn_kernel(q_ref, k_ref, v_ref, o_ref, *, blk, scale):
    i = pl.program_id(2)
    q = q_ref[0]
    qh = [q[:, :MLA_PAD_DIM], q[:, MLA_PAD_DIM:]]
    row = lax.broadcasted_iota(jnp.int32, (blk, blk), 0)
    col = lax.broadcasted_iota(jnp.int32, (blk, blk), 1)
    causal_diag = col <= row
    lane = lax.broadcasted_iota(jnp.int32, (blk, LANES), 1)

    def blocks(js, diags, carry):
        starts = [pl.multiple_of(j * blk, blk) for j in js]
        kb = [k_ref[0, pl.ds(st, blk), :] for st in starts]
        vb = [v_ref[0, pl.ds(st, blk), :] for st in starts]
        nb = range(len(js))
        sc = [[_dot_nt(qh[h], kb[n][:, h * MLA_PAD_DIM:(h + 1) * MLA_PAD_DIM]) for n in nb] for h in range(2)]
        new = []
        for h in range(2):
            m, l, acc = carry[3 * h:3 * h + 3]
            s_h = [jnp.where(causal_diag, sc[h][n], -jnp.inf) if diags[n] else sc[h][n] for n in nb]
            m_new = m
            for n in nb:
                m_new = jnp.maximum(m_new, jnp.max(s_h[n], axis=1, keepdims=True))
            alpha = jnp.exp2((m - m_new) * exp2_scale)
            p = [jnp.exp2((s_h[n] - m_new) * exp2_scale) for n in nb]
            l = alpha * l
            acc = alpha * acc
            for n in nb:
                l = l + jnp.sum(p[n], axis=1, keepdims=True)
            for n in nb:
                acc = acc + _dot(p[n].astype(BF16), vb[n])
            new += [m_new, l, acc]
        return tuple(new)

    exp2_scale = scale * math.log2(math.e)
    init = (jnp.full((blk, 1), -jnp.inf, F32), jnp.zeros((blk, 1), F32), jnp.zeros((blk, LANES), F32)) * 2
    carry = lax.fori_loop(0, i // 2, lambda s, c: blocks([2 * s, 2 * s + 1], [False, False], c), init)
    carry = lax.cond(i % 2 == 1,
                     lambda c: blocks([i - 1, i], [False, True], c),
                     lambda c: blocks([i], [True], c), carry)
    out = jnp.where(lane < HEAD_DIM, carry[2] / carry[1], carry[5] / carry[4])
    o_ref[0] = out.astype(o_ref.dtype)


def _mla_attention(q, k, v, blk):
    b, s, _ = q.shape
    pairs = MLA_HEADS // 2
    kern = functools.partial(_mla_attn_kernel, blk=blk, scale=(MLA_NOPE_DIM + MLA_ROPE_DIM) ** -0.5)
    return pl.pallas_call(
        kern,
        grid=(b, pairs, s // blk),
        in_specs=[pl.BlockSpec((1, blk, 2 * MLA_PAD_DIM), lambda bi, p, i: (bi, i, p)),
                  pl.BlockSpec((1, s, 2 * MLA_PAD_DIM), lambda bi, p, i: (bi, 0, p)),
                  pl.BlockSpec((1, s, LANES), lambda bi, p, i: (bi, 0, p))],
        out_specs=pl.BlockSpec((1, blk, LANES), lambda bi, p, i: (bi, i, p)),
        out_shape=jax.ShapeDtypeStruct((b, s, MLA_WIDTH), BF16),
        compiler_params=_params(("parallel", "parallel", "arbitrary")),
        name="mla_attn",
    )(q, k, v)


def _mm(a, b, dot=_dot):
    return dot(a.astype(BF16), b.astype(BF16))


def _rwkv_chunk_kernel(r_ref, lw_ref, k_ref, v_ref, a_ref, g_ref, kk_ref, ka_ref, rk_ref, gng_ref, gnb_ref,
                       y_ref, state_ref, *, chunk, n_sub, group):
    c = chunk
    n = HEAD_DIM
    pairs = RWKV_HEADS // 2

    @pl.when(pl.program_id(1) == 0)
    def _():
        state_ref[...] = jnp.zeros(state_ref.shape, F32)

    rowi = lax.broadcasted_iota(jnp.int32, (c, c), 0)
    coli = lax.broadcasted_iota(jnp.int32, (c, c), 1)
    tri_incl = jnp.where(rowi >= coli, 1.0, 0.0).astype(BF16)
    assert 2 * c == LANES and n == c
    t_row = lax.broadcasted_iota(jnp.int32, (c, LANES), 0)
    lane = lax.broadcasted_iota(jnp.int32, (c, LANES), 1)
    t_col = jnp.where(lane < c, lane, lane - c)
    strict = t_col < t_row
    incl = t_col <= t_row
    eye2 = jnp.where(t_col == t_row, 1.0, 0.0).astype(F32)
    m0 = lane < n
    own = jnp.concatenate([m0, jnp.logical_not(m0)], axis=0)
    lr = lax.broadcasted_iota(jnp.int32, (LANES, LANES), 0)
    lc = lax.broadcasted_iota(jnp.int32, (LANES, LANES), 1)
    blockdiag = (lr >= n) == (lc >= n)

    def stack(x):
        return jnp.where(own, jnp.concatenate([x, x], axis=0), 0.0)

    def head_sum(x):
        s0 = jnp.sum(jnp.where(m0, x, 0.0), axis=1, keepdims=True)
        s1 = jnp.sum(jnp.where(m0, 0.0, x), axis=1, keepdims=True)
        return jnp.where(m0, s0, s1)

    P = range(pairs)
    ln = [slice(p * LANES, (p + 1) * LANES) for p in P]
    for first in range(0, n_sub, group):
        subs = range(first, first + group)
        chains = [(sub, p) for sub in subs for p in P]
        g_end, KK, A, B, R, K, V, bonus = {}, {}, {}, {}, {}, {}, {}, {}
        for sub in subs:
            rows = pl.ds(sub * c, c)
            r = r_ref[0, rows, :]
            logw = lw_ref[0, rows, :]
            k = k_ref[0, rows, :]
            v = v_ref[0, rows, :]
            a = a_ref[0, rows, :]
            kk = k * kk_ref[...]
            kp = k * (1.0 + (a - 1.0) * ka_ref[...])
            bon = r * kp * rk_ref[...]
            l3 = _split_bf16(logw, 3)
            cum = _dot(tri_incl, l3[0]) + _dot(tri_incl, l3[1]) + _dot(tri_incl, l3[2])
            g_in = jnp.exp(cum)
            g_ex = jnp.exp(cum - logw)
            g_inv = jnp.exp(-cum)
            rt = r * g_in
            kt = kp * g_inv
            ag = a * g_inv
            g_end[sub] = g_in[c - 1:c, :]
            for p in P:
                ch = (sub, p)
                x = kk[:, ln[p]]
                x = x * lax.rsqrt(jnp.maximum(head_sum(jnp.square(x)), 1e-12))
                A[ch] = -x * g_ex[:, ln[p]]
                B[ch] = x * ag[:, ln[p]]
                R[ch] = rt[:, ln[p]]
                K[ch] = kt[:, ln[p]]
                V[ch] = v[:, ln[p]]
                bonus[ch] = head_sum(bon[:, ln[p]]) * V[ch]
        big = {ch: _mm(jnp.concatenate([A[ch], R[ch]], axis=0),
                       jnp.concatenate([stack(B[ch]), stack(K[ch])], axis=0), _dot_nt) for ch in chains}
        a_ab = {ch: jnp.where(strict, big[ch][:c, :LANES], 0.0) for ch in chains}
        a_ak = {ch: jnp.where(strict, big[ch][:c, LANES:], 0.0) for ch in chains}
        a_rb = {ch: jnp.where(incl, big[ch][c:, :LANES], 0.0) for ch in chains}
        a_rk = {ch: jnp.where(incl, big[ch][c:, LANES:], 0.0) for ch in chains}
        T = {ch: eye2 + a_ab[ch] for ch in chains}
        Lp = {ch: _mm(a_ab[ch], stack(a_ab[ch])) for ch in chains}
        for _ in range(int(math.log2(c)) - 2):
            X = {ch: _mm(jnp.concatenate([T[ch], Lp[ch]], axis=0), stack(Lp[ch])) for ch in chains}
            T = {ch: T[ch] + X[ch][:c] for ch in chains}
            Lp = {ch: X[ch][c:] for ch in chains}
        T = {ch: T[ch] + _mm(T[ch], stack(Lp[ch])) for ch in chains}
        akv = {ch: _mm(a_ak[ch], stack(V[ch])) for ch in chains}
        W = {ch: _mm(T[ch], jnp.concatenate([stack(A[ch]), stack(akv[ch])], axis=1)) for ch in chains}
        W1 = {ch: W[ch][:, :LANES] for ch in chains}
        W2 = {ch: W[ch][:, LANES:] for ch in chains}
        Pm = {ch: jnp.where(blockdiag, _mm(W1[ch], B[ch], _dot_tn), 0.0) for ch in chains}
        Nt = {ch: jnp.where(blockdiag, _mm(jnp.concatenate([W2[ch], V[ch]], axis=0),
                                           jnp.concatenate([B[ch], K[ch]], axis=0), _dot_tn), 0.0) for ch in chains}
        Qe = {ch: R[ch] + _mm(a_rb[ch], stack(W1[ch])) for ch in chains}
        Yl = {ch: _mm(jnp.concatenate([a_rb[ch], a_rk[ch]], axis=1),
                      jnp.concatenate([stack(W2[ch]), stack(V[ch])], axis=0)) for ch in chains}
        for sub in subs:
            cs = [(sub, p) for p in P]
            H0T = [state_ref[p] for p in P]
            HP = [_mm(H0T[p], Pm[ch]) for p, ch in enumerate(cs)]
            Y = [_mm(Qe[ch], H0T[p], _dot_nt) + Yl[ch] for p, ch in enumerate(cs)]
            for p, ch in enumerate(cs):
                state_ref[p] = (H0T[p] + HP[p] + Nt[ch]) * g_end[sub][:, ln[p]]
            ys = []
            for p, ch in enumerate(cs):
                yc = Y[p] - head_sum(Y[p]) * (1.0 / n)
                var = head_sum(jnp.square(yc)) * (1.0 / n)
                ys.append((yc * lax.rsqrt(var + RWKV_GN_EPS), bonus[ch]))
            yn = jnp.concatenate([t[0] for t in ys], axis=1)
            bo = jnp.concatenate([t[1] for t in ys], axis=1)
            rows = pl.ds(sub * c, c)
            y_ref[0, rows, :] = (yn * gng_ref[...] + gnb_ref[...] + bo) * g_ref[0, rows, :]


def _rwkv_chunked(r, logw, k, v, a, g, k_k, k_a, r_k, gn_g, gn_b, chunk, n_sub, group):
    b, s, width = r.shape
    tm = chunk * n_sub
    seq = pl.BlockSpec((1, tm, width), lambda bi, j: (bi, j, 0))
    par = pl.BlockSpec((1, width), lambda bi, j: (0, 0))
    kern = functools.partial(_rwkv_chunk_kernel, chunk=chunk, n_sub=n_sub, group=group)
    return pl.pallas_call(
        kern,
        grid=(b, s // tm),
        in_specs=[seq] * 6 + [par] * 5,
        out_specs=seq,
        out_shape=jax.ShapeDtypeStruct((b, s, width), F32),
        scratch_shapes=[pltpu.VMEM((RWKV_HEADS // 2, LANES, LANES), F32)],
        compiler_params=_params(("arbitrary", "arbitrary")),
        name="rwkv_chunk",
    )(r, logw, k, v, a, g, k_k, k_a, r_k, gn_g, gn_b)


def _layer_norm(x, g, b):
    mu = jnp.mean(x, axis=-1, keepdims=True)
    xc = x - mu
    var = jnp.mean(jnp.square(xc), axis=-1, keepdims=True)
    return xc * lax.rsqrt(var + LN_EPS) * g + b


def _mix_ffn_kernel(x_ref, sb_ref, mla_ref, y_ref, wosb_ref, womla_ref, worw_ref, ln1g_ref, ln1b_ref,
                    wact_ref, wgate_ref, convw_ref, convb_ref, wdown_ref, lng_ref, lnb_ref,
                    o_ref, hid_ref, shift_ref, carry_ref, *, alpha, fc):
    tm = x_ref.shape[1]
    d_ff = wact_ref.shape[1]

    @pl.when(pl.program_id(1) == 0)
    def _():
        carry_ref[...] = jnp.zeros(carry_ref.shape, F32)

    mix = _dot(sb_ref[0], wosb_ref[...]) + _dot(mla_ref[0], womla_ref[...])
    mix = mix + _dot(y_ref[0].astype(BF16), worw_ref[...])
    x = _layer_norm(alpha * x_ref[0] + mix, ln1g_ref[...], ln1b_ref[...])
    xb = x.astype(BF16)
    for c in range(d_ff // fc):
        cols = slice(c * fc, (c + 1) * fc)
        u = _dot(xb, wact_ref[:, cols])
        shift_ref[0:SUBLANES, :] = carry_ref[:, cols]
        shift_ref[SUBLANES:, :] = u
        carry_ref[:, cols] = u[tm - SUBLANES:, :]
        prev1 = shift_ref[pl.ds(SUBLANES - 1, tm), :]
        prev2 = shift_ref[pl.ds(SUBLANES - 2, tm), :]
        conv = (prev2 * convw_ref[0:1, cols] + prev1 * convw_ref[1:2, cols] + u * convw_ref[2:3, cols]
                + convb_ref[:, cols])
        gate = _dot(xb, wgate_ref[:, cols])
        gelu = 0.5 * conv * (1.0 + lax.erf(conv * math.sqrt(0.5)))
        hid_ref[:, cols] = (gelu * gate).astype(BF16)
    down = _dot(hid_ref[...], wdown_ref[...])
    o_ref[0] = _layer_norm(alpha * x + down, lng_ref[...], lnb_ref[...])


def _mix_ffn(x, o_sb, o_mla, y, weights, alpha, tm, fc):
    b, s, d = x.shape
    d_ff = weights[5].shape[1]
    const = lambda w: pl.BlockSpec(w.shape, lambda bi, j: (0, 0), pipeline_mode=pl.Buffered(1))
    tile = lambda n: pl.BlockSpec((1, tm, n), lambda bi, j: (bi, j, 0))
    return pl.pallas_call(
        functools.partial(_mix_ffn_kernel, alpha=alpha, fc=fc),
        grid=(b, s // tm),
        in_specs=[tile(d), tile(SB_WIDTH), tile(MLA_WIDTH), tile(RWKV_WIDTH)] + [const(w) for w in weights],
        out_specs=tile(d),
        out_shape=jax.ShapeDtypeStruct((b, s, d), F32),
        scratch_shapes=[pltpu.VMEM((tm, d_ff), BF16),
                        pltpu.VMEM((tm + SUBLANES, fc), F32),
                        pltpu.VMEM((SUBLANES, d_ff), F32)],
        compiler_params=_params(("arbitrary", "arbitrary")),
        name="mix_ffn",
    )(x, o_sb, o_mla, y, *weights)


def _rope_tables(s):
    half = MLA_ROPE_DIM // 2
    inv_freq = 1.0 / (ROPE_THETA ** (jnp.arange(0, MLA_ROPE_DIM, 2, dtype=F32) / MLA_ROPE_DIM))
    ang = jnp.arange(s, dtype=F32)[:, None] * inv_freq[None, :]
    cos, sin = jnp.cos(ang), jnp.sin(ang)
    cos2 = jnp.concatenate([cos, cos], axis=-1)
    sin2 = jnp.concatenate([-sin, sin], axis=-1)
    pad = jnp.zeros((s, MLA_PAD_DIM - MLA_NOPE_DIM - MLA_ROPE_DIM), F32)
    qcos = jnp.tile(jnp.concatenate([jnp.ones((s, MLA_NOPE_DIM), F32), cos2, pad], axis=-1), (1, MLA_HEADS))
    qsin = jnp.tile(jnp.concatenate([jnp.zeros((s, MLA_NOPE_DIM), F32), sin2, pad], axis=-1), (1, MLA_HEADS))
    ktab = jnp.concatenate([cos2, sin2], axis=-1)
    del half
    return qcos, qsin, ktab


def _swap_halves(w):
    half = w.shape[-1] // 2
    return jnp.concatenate([w[..., half:], w[..., :half]], axis=-1)


def _mla_weights(w_uq, w_ukv):
    qd = MLA_NOPE_DIM + MLA_ROPE_DIM
    pad = MLA_PAD_DIM - qd
    uq = w_uq.reshape(MLA_Q_LORA, MLA_HEADS, qd)
    zq = jnp.zeros((MLA_Q_LORA, MLA_HEADS, pad), F32)
    wq = jnp.concatenate([uq, zq], axis=-1)
    wqr = jnp.concatenate([jnp.zeros((MLA_Q_LORA, MLA_HEADS, MLA_NOPE_DIM), F32),
                           _swap_halves(uq[..., MLA_NOPE_DIM:]), zq], axis=-1)
    ukv = w_ukv.reshape(MLA_KV_LORA, MLA_HEADS, MLA_NOPE_DIM + HEAD_DIM)
    wkn = jnp.concatenate([ukv[..., :MLA_NOPE_DIM],
                           jnp.zeros((MLA_KV_LORA, MLA_HEADS, MLA_PAD_DIM - MLA_NOPE_DIM), F32)], axis=-1)
    wv = ukv[..., MLA_NOPE_DIM:]
    eye = jnp.eye(MLA_ROPE_DIM, dtype=F32)[:, None, :]
    scat = jnp.concatenate([jnp.zeros((MLA_ROPE_DIM, MLA_HEADS, MLA_NOPE_DIM), F32),
                            jnp.broadcast_to(eye, (MLA_ROPE_DIM, MLA_HEADS, MLA_ROPE_DIM)),
                            jnp.zeros((MLA_ROPE_DIM, MLA_HEADS, pad), F32)], axis=-1)
    scat = jnp.concatenate([scat, scat], axis=0)
    flat = lambda w: w.reshape(w.shape[0], -1).astype(BF16)
    return flat(wq), flat(wqr), flat(wkn), flat(wv), flat(scat)


def _layer(x, w_in, mla_q_norm, mla_w_uq, mla_kv_norm, mla_w_ukv, rwkv_mu, rwkv_w0, rwkv_w2, rwkv_a0, rwkv_a2,
           rwkv_g2, rwkv_k_k, rwkv_k_a, rwkv_r_k, rwkv_gn_g, rwkv_gn_b, w_o, ln1_g, ln1_b,
           ffn_w_up, ffn_conv_w, ffn_conv_b, ffn_w_down, ln2_g, ln2_b, *, alpha, rope):
    b, s, d = x.shape
    row_tile = min(512, s)
    attn_blk = min(256, s)

    c0 = SB_COLS
    c1 = c0 + MLA_Q_LORA
    c2 = c1 + MLA_KV_LORA
    c3 = c2 + MLA_ROPE_DIM
    w_kr = w_in[:, c2:c3]
    wkr2 = jnp.concatenate([w_kr, _swap_halves(w_kr)], axis=-1)
    bf = lambda w: w.astype(BF16)
    wq, wqr, wkn, wv, scat = _mla_weights(mla_w_uq, mla_w_ukv)
    zl = jnp.zeros((RWKV_DECAY_LORA, RWKV_WIDTH), F32)
    wlora = jnp.concatenate([jnp.concatenate([rwkv_w2, zl], axis=1),
                             jnp.concatenate([zl, rwkv_a2], axis=1)], axis=0)
    in_weights = [bf(w_in[:, :c0]), bf(w_in[:, c0:c1]), bf(w_in[:, c1:c2]), bf(wkr2), bf(w_in[:, c3:]),
                  mla_q_norm[None, :], mla_kv_norm[None, :], wq, wqr, wkn, wv, scat,
                  rwkv_mu[None, :], rwkv_w0[None, :], rwkv_a0[None, :], bf(wlora), bf(rwkv_g2)]
    q_sb, k_sb, v_sb, q_m, k_m, v_m, r, w, k, v, a, g = _inproj(x, in_weights, rope, row_tile)

    o_sb = _sb_attention(q_sb, k_sb, v_sb, attn_blk, attn_blk)
    o_mla = _mla_attention(q_m, k_m, v_m, attn_blk)
    n_sub = min(8, s // RWKV_CHUNK)
    y = _rwkv_chunked(r, w, k, v, a, g, rwkv_k_k[None, :], rwkv_k_a[None, :], rwkv_r_k.reshape(1, -1),
                      rwkv_gn_g[None, :], rwkv_gn_b[None, :], RWKV_CHUNK, n_sub, n_sub)

    d_ff = ffn_w_down.shape[0]
    out_weights = [bf(w_o[:SB_WIDTH]), bf(w_o[SB_WIDTH:SB_WIDTH + MLA_WIDTH]), bf(w_o[SB_WIDTH + MLA_WIDTH:]),
                   ln1_g[None, :], ln1_b[None, :], bf(ffn_w_up[:, :d_ff]), bf(ffn_w_up[:, d_ff:]), ffn_conv_w,
                   ffn_conv_b[None, :], bf(ffn_w_down), ln2_g[None, :], ln2_b[None, :]]
    return _mix_ffn(x, o_sb, o_mla, y, out_weights, alpha, row_tile, FFN_COL_CHUNK)


def kernel(x, w_in, mla_q_norm, mla_w_uq, mla_kv_norm, mla_w_ukv, rwkv_mu, rwkv_w0, rwkv_w2, rwkv_a0, rwkv_a2, rwkv_g2, rwkv_k_k, rwkv_k_a, rwkv_r_k, rwkv_gn_g, rwkv_gn_b, w_o, ln1_g, ln1_b, ffn_w_up, ffn_conv_w, ffn_conv_b, ffn_w_down, ln2_g, ln2_b):
    depth = w_in.shape[0]
    alpha = (2 * depth) ** 0.25
    rope = _rope_tables(x.shape[1])
    weights = (w_in, mla_q_norm, mla_w_uq, mla_kv_norm, mla_w_ukv, rwkv_mu, rwkv_w0, rwkv_w2, rwkv_a0, rwkv_a2,
               rwkv_g2, rwkv_k_k, rwkv_k_a, rwkv_r_k, rwkv_gn_g, rwkv_gn_b, w_o, ln1_g, ln1_b,
               ffn_w_up, ffn_conv_w, ffn_conv_b, ffn_w_down, ln2_g, ln2_b)
    for layer in range(depth):
        x = _layer(x, *(w[layer] for w in weights), alpha=alpha, rope=rope)
    return x
```
